```python
import math
import jax
import jax.numpy as jnp
from jax import lax
import numpy as np

D_MODEL = 2048
BATCH = 4
SEQ = 4096
DEPTH = 4

SB_WIDTH = D_MODEL // 2
SB_HEAD_DIM = 128
SB_HEADS = SB_WIDTH // SB_HEAD_DIM
QBLOCK = 128
RW_WIDTH = D_MODEL - SB_WIDTH
RW_HEAD_DIM = 64
RW_HEADS = RW_WIDTH // RW_HEAD_DIM
RW_DECAY_LORA = 64
RW_A_LORA = 64
RW_V_LORA = 32
RW_GATE_LORA = 160
RW_LNX_EPS = 64e-5
RW_KK_EPS = 1e-12
GDN_K_HEADS = 16
GDN_V_HEADS = 32
GDN_HEAD_DIM = 128
GDN_KEY_WIDTH = GDN_K_HEADS * GDN_HEAD_DIM
GDN_VAL_WIDTH = GDN_V_HEADS * GDN_HEAD_DIM
GDN_CONV_CH = 2 * GDN_KEY_WIDTH + GDN_VAL_WIDTH
GDN_CONV = 4
GDN_CHUNK = 64
GDN_NORM_EPS = 1e-6
GDN_QK_EPS = 1e-6
D_FF = 4 * D_MODEL
DN_ALPHA = (2 * DEPTH) ** 0.25
DN_BETA = (8 * DEPTH) ** -0.25
LN_EPS = 1e-5
N_EVEN = (DEPTH + 1) // 2
N_ODD = DEPTH // 2
EV_A_COLS = 3 * SB_WIDTH
EV_B_COLS = 3 * RW_WIDTH + RW_DECAY_LORA + RW_A_LORA + RW_GATE_LORA
EV_IN = EV_A_COLS + EV_B_COLS
OD_IN = GDN_CONV_CH + GDN_VAL_WIDTH + 2 * GDN_V_HEADS

kernel_name = 'hybrid_sb_rwkv7_gdn_deepnorm_trunk'


def split_cols(x, sizes):
    return jnp.split(x, [int(i) for i in np.cumsum(sizes)[:-1]], axis=-1)


def to_heads(t, head_dim):
    return t.reshape(t.shape[:-1] + (t.shape[-1] // head_dim, head_dim))


def layer_norm(x, g, b):
    xf = x.astype(jnp.float32)
    mu = jnp.mean(xf, axis=-1, keepdims=True)
    var = jnp.mean(jnp.square(xf - mu), axis=-1, keepdims=True)
    return ((xf - mu) * lax.rsqrt(var + LN_EPS) * g + b).astype(x.dtype)


def l2_normalize(x, eps):
    return x * lax.rsqrt(jnp.sum(x * x, axis=-1, keepdims=True) + eps)


def token_shift(p):
    return jnp.pad(p, ((0, 0), (1, 0), (0, 0)))[:, :-1]


def causal_dwconv(x, w):
    return lax.conv_general_dilated(
        x, w[:, None, :].astype(x.dtype), window_strides=(1,),
        padding=[(w.shape[0] - 1, 0)], dimension_numbers=('NWC', 'WIO', 'NWC'),
        feature_group_count=x.shape[-1])


def stick_breaking_attention(q, k, v):
    bsz, t_len, h, d = q.shape
    nb = t_len // QBLOCK
    qb = q.astype(jnp.float32).reshape(bsz, nb, QBLOCK, h, d).transpose(1, 0, 3, 2, 4)
    kf = k.astype(jnp.float32).transpose(0, 2, 1, 3)
    vf = v.astype(jnp.float32).transpose(0, 2, 1, 3)
    key_pos = jnp.arange(t_len)

    def one_block(args):
        q_blk, start = args
        z = jnp.einsum('bhqd,bhsd->bhqs', q_blk, kf) * (d ** -0.5)
        mask = key_pos[None, :] < (start + jnp.arange(QBLOCK))[:, None]
        log_keep = jnp.where(mask, jax.nn.log_sigmoid(-z), 0.0)
        log_a = z + lax.cumsum(log_keep, axis=3, reverse=True)
        a = jnp.exp(jnp.where(mask, log_a, -jnp.inf))
        return jnp.einsum('bhqs,bhsd->bhqd', a, vf)

    o = lax.map(one_block, (qb, jnp.arange(nb) * QBLOCK))
    return o.transpose(1, 0, 3, 2, 4).reshape(bsz, t_len, h * d)


def rwkv7_scan(r, decay, k, v, kk, a):
    bsz, _, h, n = r.shape

    def step(S, inp):
        r_t, w_t, k_t, v_t, kk_t, a_t = inp
        s_kk = jnp.einsum('bhij,bhj->bhi', S, kk_t)
        S = (S * w_t[:, :, None, :] - s_kk[..., None] * (kk_t * a_t)[:, :, None, :]
             + v_t[..., None] * k_t[:, :, None, :])
        return S, jnp.einsum('bhij,bhj->bhi', S, r_t)

    xs = tuple(jnp.moveaxis(t, 1, 0) for t in (r, decay, k, v, kk, a))
    _, y = lax.scan(step, jnp.zeros((bsz, h, n, n), jnp.float32), xs)
    return jnp.moveaxis(y, 0, 1)


def rwkv7_group(pb, mu, w0, w_up, a0, a_up, g_up, k_k, k_a, r_k, lnx_g, lnx_b,
                v_first, v0, v_up):
    pf = pb.astype(jnp.float32)
    xs = pf + (token_shift(pf) - pf) * mu
    sizes = [RW_WIDTH] * 3 + [RW_DECAY_LORA, RW_A_LORA, RW_GATE_LORA]
    if v_first is not None:
        sizes = sizes + [RW_V_LORA]
    parts = split_cols(xs, sizes)
    r, k, v, w_lo, a_lo, g_lo = parts[:6]
    if v_first is None:
        v_first = v
    else:
        v = v + (v_first - v) * jax.nn.sigmoid(v0 + parts[6] @ v_up)
    w_log = -jax.nn.softplus(-(w0 + jnp.tanh(w_lo) @ w_up)) - 0.5
    decay = jnp.exp(-jnp.exp(w_log))
    a = jax.nn.sigmoid(a0 + a_lo @ a_up)
    g = jax.nn.sigmoid(g_lo) @ g_up
    kk = l2_normalize(to_heads(k * k_k, RW_HEAD_DIM), RW_KK_EPS)
    k = k * (1.0 + (a - 1.0) * k_a)
    rh, kh, vh = (to_heads(t, RW_HEAD_DIM) for t in (r, k, v))
    y = rwkv7_scan(rh, to_heads(decay, RW_HEAD_DIM), kh, vh, kk, to_heads(a, RW_HEAD_DIM))
    mu_y = jnp.mean(y, axis=-1, keepdims=True)
    var_y = jnp.mean(jnp.square(y - mu_y), axis=-1, keepdims=True)
    y = ((y - mu_y) * lax.rsqrt(var_y + RW_LNX_EPS)).reshape(pb.shape[:2] + (RW_WIDTH,))
    y = y * lnx_g + lnx_b
    bonus = jnp.sum(rh * kh * r_k, axis=-1, keepdims=True) * vh
    return (y + bonus.reshape(y.shape)) * g, v_first


def gated_delta_rule_chunked(q, k, v, g, beta):
    bsz, t_len, h, dk = q.shape
    dv = v.shape[-1]
    n_chunks = t_len // GDN_CHUNK

    def to_chunks(t):
        t = t.reshape((bsz, n_chunks, GDN_CHUNK, h) + t.shape[3:])
        return jnp.moveaxis(t, 3, 1)

    q, k, v, g, beta = (to_chunks(t) for t in (q, k, v, g, beta))
    gc = jnp.cumsum(g, axis=-1)
    pos = jnp.arange(GDN_CHUNK)
    lower = pos[:, None] >= pos[None, :]
    strict = pos[:, None] > pos[None, :]
    decay = jnp.exp(jnp.where(lower, gc[..., :, None] - gc[..., None, :], -jnp.inf))
    k_beta = k * beta[..., None]
    kkt = jnp.einsum('bhncd,bhnsd->bhncs', k_beta, k) * decay
    tri = jnp.where(strict, kkt, 0.0) + jnp.eye(GDN_CHUNK, dtype=kkt.dtype)
    rhs = jnp.concatenate([v * beta[..., None], k_beta * jnp.exp(gc)[..., None]], axis=-1)
    sol = lax.linalg.triangular_solve(tri, rhs, left_side=True, lower=True, unit_diagonal=True)
    u, w = sol[..., :dv], sol[..., dv:]
    attn = jnp.where(lower, jnp.einsum('bhncd,bhnsd->bhncs', q, k) * decay, 0.0)

    def step(S, inp):
        q_c, k_c, u_c, w_c, gc_c, attn_c = inp
        v_new = u_c - jnp.einsum('bhcd,bhde->bhce', w_c, S)
        o_c = (jnp.einsum('bhcd,bhde->bhce', q_c * jnp.exp(gc_c)[..., None], S)
               + jnp.einsum('bhcs,bhse->bhce', attn_c, v_new))
        g_last = gc_c[..., -1:]
        S = S * jnp.exp(g_last)[..., None] + jnp.einsum(
            'bhcd,bhce->bhde', k_c * jnp.exp(g_last - gc_c)[..., None], v_new)
        return S, o_c

    xs = tuple(jnp.moveaxis(t, 2, 0) for t in (q, k, u, w, gc, attn))
    _, o = lax.scan(step, jnp.zeros((bsz, h, dk, dv), jnp.float32), xs)
    return o.transpose(1, 0, 3, 2, 4).reshape(bsz, t_len, h, dv)


def gated_deltanet(p, conv_w, a_log, dt_bias, norm_w):
    bsz, t_len, _ = p.shape
    qkv, z, b, a = split_cols(p, [GDN_CONV_CH, GDN_VAL_WIDTH, GDN_V_HEADS, GDN_V_HEADS])
    qkv = jax.nn.silu(causal_dwconv(qkv, conv_w)).astype(jnp.float32)
    q, k, v = split_cols(qkv, [GDN_KEY_WIDTH, GDN_KEY_WIDTH, GDN_VAL_WIDTH])
    rep = GDN_V_HEADS // GDN_K_HEADS
    q = jnp.repeat(l2_normalize(to_heads(q, GDN_HEAD_DIM), GDN_QK_EPS), rep, axis=2) * (GDN_HEAD_DIM ** -0.5)
    k = jnp.repeat(l2_normalize(to_heads(k, GDN_HEAD_DIM), GDN_QK_EPS), rep, axis=2)
    v = to_heads(v, GDN_HEAD_DIM)
    beta = jax.nn.sigmoid(b.astype(jnp.float32))
    g = -jnp.exp(a_log.astype(jnp.float32)) * jax.nn.softplus(a.astype(jnp.float32) + dt_bias)
    o = gated_delta_rule_chunked(q, k, v, g, beta)
    o = o * lax.rsqrt(jnp.mean(o * o, axis=-1, keepdims=True) + GDN_NORM_EPS) * norm_w
    o = o * jax.nn.silu(to_heads(z.astype(jnp.float32), GDN_HEAD_DIM))
    return o.reshape(bsz, t_len, GDN_VAL_WIDTH)


def sq_relu_mlp(x, w1, w2):
    return jnp.square(jax.nn.relu(x @ w1)) @ w2


def setup_inputs(seed: int = 0) -> dict:
    key = jax.random.key(seed)
    keys = iter(jax.random.split(key, 40))
    f32 = jnp.float32

    def normal(shape, scale):
        return jax.random.normal(next(keys), shape, f32) * scale

    def uniform(shape, lo, hi):
        return jax.random.uniform(next(keys), shape, f32, lo, hi)

    E, O, L = N_EVEN, N_ODD, DEPTH
    EV = max(N_EVEN - 1, 0)
    dt = jnp.exp(uniform((O, GDN_V_HEADS), math.log(1e-3), math.log(1e-1)))
    return {
        'x': normal((BATCH, SEQ, D_MODEL), 1.0),
        'ev_w_in': normal((E, D_MODEL, EV_IN), D_MODEL ** -0.5),
        'ev_shift': uniform((E, EV_B_COLS), 0.0, 1.0),
        'ev_w0': uniform((E, RW_WIDTH), -6.5, -1.5),
        'ev_w_up': normal((E, RW_DECAY_LORA, RW_WIDTH), RW_DECAY_LORA ** -0.5),
        'ev_a0': normal((E, RW_WIDTH), 0.1),
        'ev_a_up': normal((E, RW_A_LORA, RW_WIDTH), RW_A_LORA ** -0.5),
        'ev_g_up': normal((E, RW_GATE_LORA, RW_WIDTH), RW_GATE_LORA ** -0.5),
        'ev_k_k': 0.85 + normal((E, RW_WIDTH), 0.02),
        'ev_k_a': 1.0 + normal((E, RW_WIDTH), 0.02),
        'ev_r_k': normal((E, RW_HEADS, RW_HEAD_DIM), 0.1),
        'ev_lnx_g': 1.0 + normal((E, RW_WIDTH), 0.02),
        'ev_lnx_b': normal((E, RW_WIDTH), 0.02),
        'vres_w_down': normal((EV, D_MODEL, RW_V_LORA), D_MODEL ** -0.5),
        'vres_shift': uniform((EV, RW_V_LORA), 0.0, 1.0),
        'vres_v0': 1.0 + normal((EV, RW_WIDTH), 0.1),
        'vres_v_up': normal((EV, RW_V_LORA, RW_WIDTH), RW_V_LORA ** -0.5),
        'ev_w_out': normal((E, SB_WIDTH + RW_WIDTH, D_MODEL), (SB_WIDTH + RW_WIDTH) ** -0.5 * DN_BETA),
        'od_w_in': normal((O, D_MODEL, OD_IN), D_MODEL ** -0.5),
        'od_conv': normal((O, GDN_CONV, GDN_CONV_CH), GDN_CONV ** -0.5),
        'od_a_log': jnp.log(uniform((O, GDN_V_HEADS), 1.0, 16.0)),
        'od_dt_bias': dt + jnp.log(-jnp.expm1(-dt)),
        'od_norm_w': 1.0 + normal((O, GDN_HEAD_DIM), 0.02),
        'od_w_out': normal((O, GDN_VAL_WIDTH, D_MODEL), GDN_VAL_WIDTH ** -0.5 * DN_BETA),
        'ln1_g': 1.0 + normal((L, D_MODEL), 0.02),
        'ln1_b': normal((L, D_MODEL), 0.02),
        'mlp_w1': normal((L, D_MODEL, D_FF), D_MODEL ** -0.5),
        'mlp_w2': normal((L, D_FF, D_MODEL), D_FF ** -0.5 * DN_BETA),
        'ln2_g': 1.0 + normal((L, D_MODEL), 0.02),
        'ln2_b': normal((L, D_MODEL), 0.02),
    }


def reference(x, ev_w_in, ev_shift, ev_w0, ev_w_up, ev_a0, ev_a_up, ev_g_up, ev_k_k,
              ev_k_a, ev_r_k, ev_lnx_g, ev_lnx_b, vres_w_down, vres_shift, vres_v0,
              vres_v_up, ev_w_out, od_w_in, od_conv, od_a_log, od_dt_bias, od_norm_w,
              od_w_out, ln1_g, ln1_b, mlp_w1, mlp_w2, ln2_g, ln2_b):
    v_first = None
    for layer in range(DEPTH):
        if layer % 2 == 0:
            e = layer // 2
            if e == 0:
                w_in, mu, v0, v_up = ev_w_in[0], ev_shift[0], None, None
            else:
                w_in = jnp.concatenate([ev_w_in[e], vres_w_down[e - 1]], axis=1)
                mu = jnp.concatenate([ev_shift[e], vres_shift[e - 1]])
                v0, v_up = vres_v0[e - 1], vres_v_up[e - 1]
            p = x @ w_in
            sb_q, sb_k, sb_v = split_cols(p[..., :EV_A_COLS], [SB_WIDTH] * 3)
            o_sb = stick_breaking_attention(to_heads(sb_q, SB_HEAD_DIM),
                                            to_heads(sb_k, SB_HEAD_DIM),
                                            to_heads(sb_v, SB_HEAD_DIM))
            o_rw, v_first = rwkv7_group(p[..., EV_A_COLS:], mu, ev_w0[e], ev_w_up[e], ev_a0[e],
                                        ev_a_up[e], ev_g_up[e], ev_k_k[e], ev_k_a[e], ev_r_k[e],
                                        ev_lnx_g[e], ev_lnx_b[e], v_first, v0, v_up)
            h = jnp.concatenate([o_sb, o_rw], axis=-1).astype(x.dtype) @ ev_w_out[e]
        else:
            o = layer // 2
            h = gated_deltanet(x @ od_w_in[o], od_conv[o], od_a_log[o], od_dt_bias[o],
                               od_norm_w[o]).astype(x.dtype) @ od_w_out[o]
        x = layer_norm(DN_ALPHA * x + h.astype(x.dtype), ln1_g[layer], ln1_b[layer])
        x = layer_norm(DN_ALPHA * x + sq_relu_mlp(x, mlp_w1[layer], mlp_w2[layer]).astype(x.dtype),
                       ln2_g[layer], ln2_b[layer])
    return x
```

```python
import functools
import math

import jax
import jax.numpy as jnp
from jax import lax
from jax.experimental import pallas as pl
from jax.experimental.pallas import tpu as pltpu

F32 = jnp.float32
BF16 = jnp.bfloat16

LANES = 128
SUBLANES = 8
CHUNK = 64
VMEM_LIMIT = 48 * 1024 * 1024

SB_HEAD_DIM = 128
RW_HEAD_DIM = 64
GDN_HEAD_DIM = 128
GDN_CONV = 4
RW_LNX_EPS = 64e-5
RW_KK_EPS = 1e-12
GDN_NORM_EPS = 1e-6
GDN_QK_EPS = 1e-6
LN_EPS = 1e-5


def _params(*sem):
    return pltpu.CompilerParams(dimension_semantics=sem, vmem_limit_bytes=VMEM_LIMIT)


def _dot(a, b):
    return jnp.dot(a.astype(BF16), b.astype(BF16), preferred_element_type=F32)


def _dot_nt(a, b):
    return lax.dot_general(a.astype(BF16), b.astype(BF16), (((1,), (1,)), ((), ())),
                           preferred_element_type=F32)


def _dot_tn(a, b):
    return lax.dot_general(a.astype(BF16), b.astype(BF16), (((0,), (0,)), ((), ())),
                           preferred_element_type=F32)


def _split3(x):
    x1 = x.astype(BF16)
    r1 = x - x1.astype(F32)
    x2 = r1.astype(BF16)
    x3 = (r1 - x2.astype(F32)).astype(BF16)
    return x1, x2, x3


def _dot_exact_lhs(m01, x):
    x1, x2, x3 = _split3(x)
    d = lambda t: jnp.dot(m01, t, preferred_element_type=F32)
    return d(x1) + d(x2) + d(x3)


def _dot_exact_rhs(x, m01):
    x1, x2, x3 = _split3(x)
    d = lambda t: jnp.dot(t, m01, preferred_element_type=F32)
    return d(x1) + d(x2) + d(x3)


def _tri(n, kind):
    r = lax.broadcasted_iota(jnp.int32, (n, n), 0)
    c = lax.broadcasted_iota(jnp.int32, (n, n), 1)
    return {"ge": r >= c, "gt": r > c, "le": r <= c}[kind]


def _mm_kernel(a_ref, w_ref, o_ref, *scratch, act, nk):
    prod = jnp.dot(a_ref[...], w_ref[...], preferred_element_type=F32)

    def finish(acc):
        if act == "relu2":
            acc = jnp.square(jnp.maximum(acc, 0.0))
        o_ref[...] = acc.astype(o_ref.dtype)

    if nk == 1:
        finish(prod)
    else:
        acc_ref, = scratch
        k = pl.program_id(2)

        @pl.when(k == 0)
        def _():
            acc_ref[...] = prod

        @pl.when(jnp.logical_and(k > 0, k < nk - 1))
        def _():
            acc_ref[...] += prod

        @pl.when(k == nk - 1)
        def _():
            finish(acc_ref[...] + prod)


def _pick(n, cands):
    for c in cands:
        if n % c == 0:
            return c
    raise ValueError(f"no tile for {n}")


def matmul(a, w, *, act=None, out_dtype=F32, name="mm"):
    m, k = a.shape
    k2, n = w.shape
    assert k == k2
    tm = _pick(m, (1024, 512, 256, 128, 64, 32, 16, 8))
    tn = _pick(n, (1024, 512, 256, 128))
    tk = _pick(k, (2048, 1024, 512, 256, 128))
    nk = k // tk
    scratch = [pltpu.VMEM((tm, tn), F32)] if nk > 1 else []
    return pl.pallas_call(
        functools.partial(_mm_kernel, act=act, nk=nk),
        out_shape=jax.ShapeDtypeStruct((m, n), out_dtype),
        grid=(m // tm, n // tn, nk),
        in_specs=[pl.BlockSpec((tm, tk), lambda i, j, kk: (i, kk)),
                  pl.BlockSpec((tk, tn), lambda i, j, kk: (kk, j))],
        out_specs=pl.BlockSpec((tm, tn), lambda i, j, kk: (i, j)),
        scratch_shapes=scratch,
        compiler_params=_params("parallel", "parallel", "arbitrary"),
        name=name,
    )(a, w)


def _add_ln_kernel(x_ref, h_ref, g_ref, b_ref, o_ref, ob_ref, *, alpha):
    y = alpha * x_ref[...] + h_ref[...].astype(F32)
    mu = jnp.mean(y, axis=-1, keepdims=True)
    yc = y - mu
    var = jnp.mean(jnp.square(yc), axis=-1, keepdims=True)
    out = yc * lax.rsqrt(var + LN_EPS) * g_ref[...] + b_ref[...]
    o_ref[...] = out
    ob_ref[...] = out.astype(BF16)


def add_ln(x, h, g, b, alpha):
    m, d = x.shape
    tr = _pick(m, (256, 128, 64, 32, 16, 8))
    row = pl.BlockSpec((tr, d), lambda i: (i, 0))
    vec = pl.BlockSpec((1, d), lambda i: (0, 0))
    return pl.pallas_call(
        functools.partial(_add_ln_kernel, alpha=alpha),
        out_shape=(jax.ShapeDtypeStruct((m, d), F32), jax.ShapeDtypeStruct((m, d), BF16)),
        grid=(m // tr,),
        in_specs=[row, row, vec, vec],
        out_specs=(row, row),
        compiler_params=_params("parallel"),
        name="add_ln",
    )(x, h, g.reshape(1, d), b.reshape(1, d))


def _sb_kernel(q_ref, k_ref, v_ref, o_ref, *, tq, scale):
    qi = pl.program_id(2)
    q = q_ref[0]
    d = q.shape[-1]
    suffix = _tri(tq, "ge").astype(BF16)
    keep = _tri(tq, "gt")

    def block(j, carry, diagonal):
        acc, c = carry
        start = pl.multiple_of(j * tq, tq)
        kb = k_ref[0, pl.ds(start, tq), :]
        vb = v_ref[0, pl.ds(start, tq), :]
        z = _dot_nt(q, kb) * scale
        ls = jnp.minimum(-z, 0.0) - jnp.log1p(jnp.exp(-jnp.abs(z)))
        if diagonal:
            ls = jnp.where(keep, ls, 0.0)
        hi = ls.astype(BF16)
        lo = (ls - hi.astype(F32)).astype(BF16)
        rsum = (jnp.dot(hi, suffix, preferred_element_type=F32)
                + jnp.dot(lo, suffix, preferred_element_type=F32))
        a = jnp.exp(z + rsum + c)
        if diagonal:
            a = jnp.where(keep, a, 0.0)
        acc = acc + jnp.dot(a.astype(BF16), vb, preferred_element_type=F32)
        return acc, c + rsum[:, :1]

    carry = (jnp.zeros((tq, d), F32), jnp.zeros((tq, 1), F32))
    carry = block(qi, carry, True)
    acc, _ = lax.fori_loop(0, qi, lambda i, cr: block(qi - 1 - i, cr, False), carry)
    o_ref[0] = acc.astype(o_ref.dtype)


def sb_attention(p, n_heads, *, tq=256):
    bsz, t_len, _ = p.shape
    d = SB_HEAD_DIM
    tq = min(tq, t_len)
    assert t_len % tq == 0
    return pl.pallas_call(
        functools.partial(_sb_kernel, tq=tq, scale=d ** -0.5),
        out_shape=jax.ShapeDtypeStruct((bsz, t_len, n_heads * d), BF16),
        grid=(bsz, n_heads, t_len // tq),
        in_specs=[pl.BlockSpec((1, tq, d), lambda b, h, i: (b, i, h)),
                  pl.BlockSpec((1, t_len, d), lambda b, h, i: (b, 0, n_heads + h)),
                  pl.BlockSpec((1, t_len, d), lambda b, h, i: (b, 0, 2 * n_heads + h))],
        out_specs=pl.BlockSpec((1, tq, d), lambda b, h, i: (b, i, h)),
        compiler_params=_params("parallel", "parallel", "arbitrary"),
        name="sb_attention",
    )(p, p, p)


def _tri_inv_kernel(l_ref, t_ref, *, c):
    nb = c // SUBLANES
    sub = lax.broadcasted_iota(jnp.int32, (SUBLANES, LANES), 0)
    zero = jnp.zeros((SUBLANES, LANES), F32)
    for t in range(c):
        tb = t // SUBLANES
        accs = [zero] * (tb + 1)
        accs[tb] = jnp.where(sub == (t % SUBLANES), 1.0, 0.0)
        for j in range(t):
            lt = jnp.broadcast_to(l_ref[t, j:j + 1, :], (SUBLANES, LANES))
            for cb in range(j // SUBLANES + 1):
                accs[cb] = accs[cb] - lt * t_ref[j, cb * SUBLANES:(cb + 1) * SUBLANES, :]
        for cb in range(nb):
            t_ref[t, cb * SUBLANES:(cb + 1) * SUBLANES, :] = accs[cb] if cb <= tb else zero


def tri_inverse(l):
    shape = l.shape
    c = shape[-1]
    n_sys = math.prod(shape[:-2])
    n = -(-n_sys // LANES) * LANES
    lt = jnp.pad(l.reshape(n_sys, c * c), ((0, n - n_sys), (0, 0))).T.reshape(c, c, n)
    blk = pl.BlockSpec((c, c, LANES), lambda i: (0, 0, i))
    tt = pl.pallas_call(
        functools.partial(_tri_inv_kernel, c=c),
        out_shape=jax.ShapeDtypeStruct((c, c, n), F32),
        grid=(n // LANES,),
        in_specs=[blk],
        out_specs=blk,
        compiler_params=_params("parallel"),
        name="tri_inverse",
    )(lt)
    return tt.reshape(c * c, n).T[:n_sys].reshape(shape)


def _rwkv_decays(lw, kk, a):
    c = lw.shape[0]
    g = _dot_exact_lhs(_tri(c, "ge").astype(BF16), lw)
    b = a * kk
    kq = kk * jnp.exp(g - lw)
    bk = b * jnp.exp(-g)
    return g, b, kq, bk


def _rwkv_l_kernel(lw_ref, kk_ref, a_ref, l_ref, *, nh):
    c = lw_ref.shape[2]
    strict = _tri(c, "gt")

    def head(h, carry):
        _, _, kq, bk = _rwkv_decays(lw_ref[0, h], kk_ref[0, h], a_ref[0, h])
        l_ref[0, 0, h] = jnp.where(strict, _dot_nt(kq, bk), 0.0)
        return carry

    lax.fori_loop(0, nh, head, 0)


def _rwkv_scan_kernel(r_ref, lw_ref, k_ref, v_ref, kk_ref, a_ref, t_ref, y_ref, s_ref, *, nh):
    c = lw_ref.shape[2]

    @pl.when(pl.program_id(1) == 0)
    def _():
        s_ref[...] = jnp.zeros_like(s_ref)

    strict = _tri(c, "gt")
    incl = _tri(c, "ge")

    def head(h, carry):
        r, lw, k, v, kk, a = (ref[0, h] for ref in (r_ref, lw_ref, k_ref, v_ref, kk_ref, a_ref))
        tm = t_ref[0, 0, h]
        g, b, kq, bk = _rwkv_decays(lw, kk, a)
        rq = r * jnp.exp(g)
        kk_ = k * jnp.exp(-g)
        a_uk = jnp.where(strict, _dot_nt(kq, kk_), 0.0)
        a_yk = jnp.where(incl, _dot_nt(rq, kk_), 0.0)
        a_yb = jnp.where(incl, _dot_nt(rq, bk), 0.0)
        wq = _dot(tm, kq)
        ut = _dot(tm, _dot(a_uk, v))
        qeff = rq - _dot(a_yb, wq)
        yl = _dot(a_yk, v) - _dot(a_yb, ut)
        glast = g[c - 1:c, :]
        tail = jnp.exp(glast - g)
        s = s_ref[h]
        u = _dot_nt(wq, s) + ut
        y_ref[0, h] = _dot_nt(qeff, s) + yl
        s_ref[h] = s * jnp.exp(glast) + _dot_tn(v, k * tail) - _dot_tn(u, b * tail)
        return carry

    lax.fori_loop(0, nh, head, 0)


def rwkv7_recurrence(r, lw, k, v, kk, a):
    bsz, nh, t_len, n = r.shape
    c = min(CHUNK, t_len)
    nc = t_len // c
    seq = pl.BlockSpec((1, nh, c, n), lambda b, i: (b, 0, i, 0))
    mat = pl.BlockSpec((1, 1, nh, c, c), lambda b, i: (b, i, 0, 0, 0))
    l = pl.pallas_call(
        functools.partial(_rwkv_l_kernel, nh=nh),
        out_shape=jax.ShapeDtypeStruct((bsz, nc, nh, c, c), F32),
        grid=(bsz, nc),
        in_specs=[seq, seq, seq],
        out_specs=mat,
        compiler_params=_params("parallel", "parallel"),
        name="rwkv_l",
    )(lw, kk, a)
    tinv = tri_inverse(l)
    return pl.pallas_call(
        functools.partial(_rwkv_scan_kernel, nh=nh),
        out_shape=jax.ShapeDtypeStruct((bsz, nh, t_len, n), F32),
        grid=(bsz, nc),
        in_specs=[seq] * 6 + [mat],
        out_specs=seq,
        scratch_shapes=[pltpu.VMEM((nh, n, n), F32)],
        compiler_params=_params("parallel", "arbitrary"),
        name="rwkv_scan",
    )(r, lw, k, v, kk, a, tinv)


def _gdn_decay(gcol_ref, grow_ref, j):
    c = gcol_ref.shape[3]
    gc_col = _dot_exact_lhs(_tri(c, "ge").astype(BF16), gcol_ref[0, 0, 0])
    gc_row = _dot_exact_rhs(grow_ref[0, 0, 0], _tri(c, "le").astype(BF16))
    gcc = gc_col[:, j:j + 1]
    diff = gcc - gc_row[j:j + 1, :]
    decay = jnp.where(_tri(c, "ge"), jnp.exp(jnp.minimum(diff, 0.0)), 0.0)
    return decay, gcc


def _gdn_l_kernel(k_ref, beta_ref, gcol_ref, grow_ref, l_ref, *, hv, rep):
    c = k_ref.shape[1]
    dh = GDN_HEAD_DIM
    strict = _tri(c, "gt")
    for j in range(hv):
        kh = j // rep
        k = k_ref[0, :, kh * dh:(kh + 1) * dh]
        beta = beta_ref[0, 0, 0][:, j:j + 1]
        decay, _ = _gdn_decay(gcol_ref, grow_ref, j)
        l_ref[0, 0, j] = jnp.where(strict, _dot_nt(k * beta, k) * decay, 0.0)


def _gdn_scan_kernel(q_ref, k_ref, v_ref, beta_ref, gcol_ref, grow_ref, t_ref, o_ref, s_ref, *, hv, rep):
    c = k_ref.shape[1]
    dh = GDN_HEAD_DIM

    @pl.when(pl.program_id(2) == 0)
    def _():
        s_ref[...] = jnp.zeros_like(s_ref)

    incl = _tri(c, "ge")
    for j in range(hv):
        kh = j // rep
        q = q_ref[0, :, kh * dh:(kh + 1) * dh]
        k = k_ref[0, :, kh * dh:(kh + 1) * dh]
        v = v_ref[0, :, j * dh:(j + 1) * dh]
        beta = beta_ref[0, 0, 0][:, j:j + 1]
        tm = t_ref[0, 0, j]
        decay, gcc = _gdn_decay(gcol_ref, grow_ref, j)
        k_beta = k * beta
        egc = jnp.exp(gcc)
        u = _dot(tm, v * beta)
        w = _dot(tm, k_beta * egc)
        attn = jnp.where(incl, _dot_nt(q, k) * decay, 0.0)
        s = s_ref[j]
        v_new = u - _dot(w, s)
        o_ref[0, :, j * dh:(j + 1) * dh] = _dot(q * egc, s) + _dot(attn, v_new)
        g_last = gcc[c - 1:c, :]
        s_ref[j] = s * jnp.exp(g_last) + _dot_tn(k * jnp.exp(g_last - gcc), v_new)


def gdn_recurrence(q, k, v, g, beta, *, hv_group=8):
    bsz, t_len, kw = q.shape
    dh = GDN_HEAD_DIM
    hk = kw // dh
    hv = v.shape[-1] // dh
    rep = hv // hk
    hvg = min(hv_group, hv)
    ng = hv // hvg
    c = min(CHUNK, t_len)
    nc = t_len // c
    col = lambda x: x.reshape(bsz, nc, c, ng, hvg).transpose(0, 3, 1, 2, 4)
    g_col, beta_col = col(g), col(beta)
    g_row = g_col.transpose(0, 1, 2, 4, 3)
    qk_spec = pl.BlockSpec((1, c, hvg // rep * dh), lambda b, h, i: (b, i, h))
    v_spec = pl.BlockSpec((1, c, hvg * dh), lambda b, h, i: (b, i, h))
    col_spec = pl.BlockSpec((1, 1, 1, c, hvg), lambda b, h, i: (b, h, i, 0, 0))
    row_spec = pl.BlockSpec((1, 1, 1, hvg, c), lambda b, h, i: (b, h, i, 0, 0))
    mat = pl.BlockSpec((1, 1, hvg, c, c), lambda b, h, i: (b, i, h, 0, 0))
    l = pl.pallas_call(
        functools.partial(_gdn_l_kernel, hv=hvg, rep=rep),
        out_shape=jax.ShapeDtypeStruct((bsz, nc, hv, c, c), F32),
        grid=(bsz, ng, nc),
        in_specs=[qk_spec, col_spec, col_spec, row_spec],
        out_specs=mat,
        compiler_params=_params("parallel", "parallel", "parallel"),
        name="gdn_l",
    )(k, beta_col, g_col, g_row)
    tinv = tri_inverse(l)
    return pl.pallas_call(
        functools.partial(_gdn_scan_kernel, hv=hvg, rep=rep),
        out_shape=jax.ShapeDtypeStruct((bsz, t_len, hv * dh), F32),
        grid=(bsz, ng, nc),
        in_specs=[qk_spec, qk_spec, v_spec, col_spec, col_spec, row_spec, mat],
        out_specs=v_spec,
        scratch_shapes=[pltpu.VMEM((hvg, dh, dh), F32)],
        compiler_params=_params("parallel", "parallel", "arbitrary"),
        name="gdn_scan",
    )(q, k, v, beta_col, g_col, g_row, tinv)


def _heads(t, hd):
    return t.reshape(t.shape[:-1] + (t.shape[-1] // hd, hd))


def _l2n(x, eps):
    return x * lax.rsqrt(jnp.sum(x * x, axis=-1, keepdims=True) + eps)


def rwkv7_group(pb, mu, w0, w_up, a0, a_up, g_up, k_k, k_a, r_k, lnx_g, lnx_b, v_first, v0, v_up, sizes):
    bsz, t_len, _ = pb.shape
    prev = jnp.pad(pb, ((0, 0), (1, 0), (0, 0)))[:, :-1]
    xs = pb + (prev - pb) * mu
    offs = [0]
    for s in sizes:
        offs.append(offs[-1] + s)
    parts = [xs[..., offs[i]:offs[i + 1]] for i in range(len(sizes))]
    r, k, v, w_lo, a_lo, g_lo = parts[:6]
    if v_first is None:
        v_first = v
    else:
        v = v + (v_first - v) * jax.nn.sigmoid(v0 + parts[6] @ v_up)
    w_log = -jax.nn.softplus(-(w0 + jnp.tanh(w_lo) @ w_up)) - 0.5
    lw = -jnp.exp(w_log)
    a = jax.nn.sigmoid(a0 + a_lo @ a_up)
    g = jax.nn.sigmoid(g_lo) @ g_up
    kk = _l2n(_heads(k * k_k, RW_HEAD_DIM), RW_KK_EPS)
    k = k * (1.0 + (a - 1.0) * k_a)
    to_bh = lambda t: _heads(t, RW_HEAD_DIM).transpose(0, 2, 1, 3)
    rh, kh, vh = _heads(r, RW_HEAD_DIM), _heads(k, RW_HEAD_DIM), _heads(v, RW_HEAD_DIM)
    y = rwkv7_recurrence(to_bh(r), to_bh(lw), to_bh(k), to_bh(v), kk.transpose(0, 2, 1, 3), to_bh(a))
    y = y.transpose(0, 2, 1, 3)
    mu_y = jnp.mean(y, axis=-1, keepdims=True)
    var_y = jnp.mean(jnp.square(y - mu_y), axis=-1, keepdims=True)
    y = ((y - mu_y) * lax.rsqrt(var_y + RW_LNX_EPS)).reshape(bsz, t_len, -1)
    y = y * lnx_g + lnx_b
    bonus = jnp.sum(rh * kh * r_k, axis=-1, keepdims=True) * vh
    return (y + bonus.reshape(y.shape)) * g, v_first


def gated_deltanet(qkv, z, ba, conv_w, a_log, dt_bias, norm_w, key_width):
    bsz, t_len, _ = qkv.shape
    hv = a_log.shape[0]
    b, a = ba[..., :hv], ba[..., hv:2 * hv]
    pad = jnp.pad(qkv, ((0, 0), (GDN_CONV - 1, 0), (0, 0)))
    conv = sum(conv_w[i] * pad[:, i:i + t_len] for i in range(GDN_CONV))
    qkv = jax.nn.silu(conv)
    q, k, v = qkv[..., :key_width], qkv[..., key_width:2 * key_width], qkv[..., 2 * key_width:]
    q = _l2n(_heads(q, GDN_HEAD_DIM), GDN_QK_EPS).reshape(bsz, t_len, key_width) * (GDN_HEAD_DIM ** -0.5)
    k = _l2n(_heads(k, GDN_HEAD_DIM), GDN_QK_EPS).reshape(bsz, t_len, key_width)
    beta = jax.nn.sigmoid(b)
    g = -jnp.exp(a_log) * jax.nn.softplus(a + dt_bias)
    o = _heads(gdn_recurrence(q, k, v, g, beta), GDN_HEAD_DIM)
    o = o * lax.rsqrt(jnp.mean(o * o, axis=-1, keepdims=True) + GDN_NORM_EPS) * norm_w
    o = o * jax.nn.silu(_heads(z, GDN_HEAD_DIM))
    return o.reshape(bsz, t_len, -1)


def _pad_cols(w, mult):
    n = w.shape[1]
    return jnp.pad(w, ((0, 0), (0, (-n) % mult)))


def kernel(x, ev_w_in, ev_shift, ev_w0, ev_w_up, ev_a0, ev_a_up, ev_g_up, ev_k_k, ev_k_a, ev_r_k, ev_lnx_g, ev_lnx_b, vres_w_down, vres_shift, vres_v0, vres_v_up, ev_w_out, od_w_in, od_conv, od_a_log, od_dt_bias, od_norm_w, od_w_out, ln1_g, ln1_b, mlp_w1, mlp_w2, ln2_g, ln2_b):
    bsz, t_len, d_model = x.shape
    depth = ln1_g.shape[0]
    alpha = (2 * depth) ** 0.25
    m = bsz * t_len
    rw_width = ev_w0.shape[1]
    sb_width = ev_w_out.shape[1] - rw_width
    sb_heads = sb_width // SB_HEAD_DIM
    sb_cols = 3 * sb_width
    lora = [ev_w_up.shape[1], ev_a_up.shape[1], ev_g_up.shape[1]]
    gdn_hv = od_a_log.shape[1]
    val_width = gdn_hv * GDN_HEAD_DIM
    conv_ch = od_conv.shape[2]
    key_width = (conv_ch - val_width) // 2

    x = x.reshape(m, d_model)
    xb = x.astype(BF16)
    v_first = None
    for layer in range(depth):
        if layer % 2 == 0:
            e = layer // 2
            w_sb = ev_w_in[e][:, :sb_cols]
            w_rw = ev_w_in[e][:, sb_cols:]
            mu = ev_shift[e]
            sizes = [rw_width] * 3 + lora
            v0 = v_up = None
            if e > 0:
                w_rw = jnp.concatenate([w_rw, vres_w_down[e - 1]], axis=1)
                mu = jnp.concatenate([mu, vres_shift[e - 1]])
                sizes = sizes + [vres_w_down.shape[2]]
                v0, v_up = vres_v0[e - 1], vres_v_up[e - 1]
            n_rw = w_rw.shape[1]
            p_sb = matmul(xb, w_sb.astype(BF16), out_dtype=BF16, name="proj_sb")
            p_rw = matmul(xb, _pad_cols(w_rw, 512).astype(BF16), name="proj_rw")[:, :n_rw]
            o_sb = sb_attention(p_sb.reshape(bsz, t_len, sb_cols), sb_heads)
            o_rw, v_first = rwkv7_group(p_rw.reshape(bsz, t_len, n_rw), mu, ev_w0[e], ev_w_up[e], ev_a0[e],
                                        ev_a_up[e], ev_g_up[e], ev_k_k[e], ev_k_a[e], ev_r_k[e],
                                        ev_lnx_g[e], ev_lnx_b[e], v_first, v0, v_up, sizes)
            mixed = jnp.concatenate([o_sb, o_rw.astype(BF16)], axis=-1).reshape(m, -1)
            h = matmul(mixed, ev_w_out[e].astype(BF16), name="proj_out")
        else:
            o = layer // 2
            w_main = od_w_in[o][:, :conv_ch + val_width]
            w_ba = od_w_in[o][:, conv_ch + val_width:]
            p_main = matmul(xb, w_main.astype(BF16), name="proj_gdn")
            p_ba = matmul(xb, _pad_cols(w_ba, LANES).astype(BF16), name="proj_gdn_gates")[:, :2 * gdn_hv]
            p_main = p_main.reshape(bsz, t_len, -1)
            mixed = gated_deltanet(p_main[..., :conv_ch], p_main[..., conv_ch:],
                                   p_ba.reshape(bsz, t_len, -1), od_conv[o], od_a_log[o],
                                   od_dt_bias[o], od_norm_w[o], key_width)
            h = matmul(mixed.astype(BF16).reshape(m, -1), od_w_out[o].astype(BF16), name="proj_out")
        x, xb = add_ln(x, h, ln1_g[layer], ln1_b[layer], alpha)
        hid = matmul(xb, mlp_w1[layer].astype(BF16), act="relu2", out_dtype=BF16, name="mlp_up")
        h = matmul(hid, mlp_w2[layer].astype(BF16), name="mlp_down")
        x, xb = add_ln(x, h, ln2_g[layer], ln2_b[layer], alpha)
    return x.reshape(bsz, t_len, d_model)
```

```python
import functools
import math

import jax
import jax.numpy as jnp
from jax import lax
from jax.experimental import pallas as pl
from jax.experimental.pallas import tpu as pltpu

F32 = jnp.float32
BF16 = jnp.bfloat16

LANES = 128
SUBLANES = 8
CHUNK = 64
VMEM_LIMIT = 48 * 1024 * 1024

SB_HEAD_DIM = 128
RW_HEAD_DIM = 64
GDN_HEAD_DIM = 128
GDN_CONV = 4
RW_LNX_EPS = 64e-5
RW_KK_EPS = 1e-12
GDN_NORM_EPS = 1e-6
GDN_QK_EPS = 1e-6
LN_EPS = 1e-5


def _params(*sem):
    return pltpu.CompilerParams(dimension_semantics=sem, vmem_limit_bytes=VMEM_LIMIT)


def _dot(a, b):
    return jnp.dot(a.astype(BF16), b.astype(BF16), preferred_element_type=F32)


def _dot_nt(a, b):
    return lax.dot_general(a.astype(BF16), b.astype(BF16), (((1,), (1,)), ((), ())),
                           preferred_element_type=F32)


def _dot_tn(a, b):
    return lax.dot_general(a.astype(BF16), b.astype(BF16), (((0,), (0,)), ((), ())),
                           preferred_element_type=F32)


def _split3(x):
    x1 = x.astype(BF16)
    r1 = x - x1.astype(F32)
    x2 = r1.astype(BF16)
    x3 = (r1 - x2.astype(F32)).astype(BF16)
    return x1, x2, x3


def _dot_exact_lhs(m01, x):
    x1, x2, x3 = _split3(x)
    d = lambda t: jnp.dot(m01, t, preferred_element_type=F32)
    return d(x1) + d(x2) + d(x3)


def _dot_exact_rhs(x, m01):
    x1, x2, x3 = _split3(x)
    d = lambda t: jnp.dot(t, m01, preferred_element_type=F32)
    return d(x1) + d(x2) + d(x3)


def _tri(n, kind):
    r = lax.broadcasted_iota(jnp.int32, (n, n), 0)
    c = lax.broadcasted_iota(jnp.int32, (n, n), 1)
    return {"ge": r >= c, "gt": r > c, "le": r <= c}[kind]


def _mm_kernel(a_ref, w_ref, o_ref, *scratch, act, nk):
    prod = jnp.dot(a_ref[...], w_ref[...], preferred_element_type=F32)

    def finish(acc):
        if act == "relu2":
            acc = jnp.square(jnp.maximum(acc, 0.0))
        o_ref[...] = acc.astype(o_ref.dtype)

    if nk == 1:
        finish(prod)
    else:
        acc_ref, = scratch
        k = pl.program_id(2)

        @pl.when(k == 0)
        def _():
            acc_ref[...] = prod

        @pl.when(jnp.logical_and(k > 0, k < nk - 1))
        def _():
            acc_ref[...] += prod

        @pl.when(k == nk - 1)
        def _():
            finish(acc_ref[...] + prod)


def _pick(n, cands):
    for c in cands:
        if n % c == 0:
            return c
    raise ValueError(f"no tile for {n}")


def matmul(a, w, *, act=None, out_dtype=F32, name="mm"):
    m, k = a.shape
    k2, n = w.shape
    assert k == k2
    tm = _pick(m, (1024, 512, 256, 128, 64, 32, 16, 8))
    tn = _pick(n, (1024, 512, 256, 128))
    tk = _pick(k, (2048, 1024, 512, 256, 128))
    nk = k // tk
    scratch = [pltpu.VMEM((tm, tn), F32)] if nk > 1 else []
    return pl.pallas_call(
        functools.partial(_mm_kernel, act=act, nk=nk),
        out_shape=jax.ShapeDtypeStruct((m, n), out_dtype),
        grid=(m // tm, n // tn, nk),
        in_specs=[pl.BlockSpec((tm, tk), lambda i, j, kk: (i, kk)),
                  pl.BlockSpec((tk, tn), lambda i, j, kk: (kk, j))],
        out_specs=pl.BlockSpec((tm, tn), lambda i, j, kk: (i, j)),
        scratch_shapes=scratch,
        compiler_params=_params("parallel", "parallel", "arbitrary"),
        name=name,
    )(a, w)


def _add_ln_kernel(x_ref, h_ref, g_ref, b_ref, o_ref, ob_ref, *, alpha):
    y = alpha * x_ref[...] + h_ref[...].astype(F32)
    mu = jnp.mean(y, axis=-1, keepdims=True)
    yc = y - mu
    var = jnp.mean(jnp.square(yc), axis=-1, keepdims=True)
    out = yc * lax.rsqrt(var + LN_EPS) * g_ref[...] + b_ref[...]
    o_ref[...] = out
    ob_ref[...] = out.astype(BF16)


def add_ln(x, h, g, b, alpha):
    m, d = x.shape
    tr = _pick(m, (256, 128, 64, 32, 16, 8))
    row = pl.BlockSpec((tr, d), lambda i: (i, 0))
    vec = pl.BlockSpec((1, d), lambda i: (0, 0))
    return pl.pallas_call(
        functools.partial(_add_ln_kernel, alpha=alpha),
        out_shape=(jax.ShapeDtypeStruct((m, d), F32), jax.ShapeDtypeStruct((m, d), BF16)),
        grid=(m // tr,),
        in_specs=[row, row, vec, vec],
        out_specs=(row, row),
        compiler_params=_params("parallel"),
        name="add_ln",
    )(x, h, g.reshape(1, d), b.reshape(1, d))


def _split_hi_lo(x):
    hi = lax.bitcast_convert_type(lax.bitcast_convert_type(x, jnp.uint32) & jnp.uint32(0xFFFF0000), F32)
    return hi.astype(BF16), (x - hi).astype(BF16)


def _sb_kernel(q_ref, k_ref, v_ref, o_ref, *, tq, scale, nh):
    qi = pl.program_id(2)
    d = SB_HEAD_DIM
    tk = 2 * tq
    r = lax.broadcasted_iota(jnp.int32, (tq, tq), 0)
    c = lax.broadcasted_iota(jnp.int32, (tq, tq), 1)
    suffix = (r >= c).astype(BF16)
    suffix2 = jnp.concatenate([suffix, suffix], axis=0)
    qs = [q_ref[0, :, h * d:(h + 1) * d] for h in range(nh)]
    streams = [(h, half) for h in range(nh) for half in (1, 0)]

    def rows(j, half):
        return pl.ds(pl.multiple_of(j * tk + half * tq, tq), tq)

    def cumsum(z, keep):
        sp = jnp.maximum(z, 0.0) + jnp.log(1.0 + jnp.exp(-jnp.abs(z)))
        if keep is not None:
            sp = jnp.where(keep, sp, 0.0)
        hi, lo = _split_hi_lo(sp)
        return jnp.dot(jnp.concatenate([hi, lo], axis=1), suffix2, preferred_element_type=F32)

    def block(j, carry, diagonal):
        accs, cins = list(carry[0]), list(carry[1])
        zs = [lax.dot_general(qs[h], k_ref[0, rows(j, half), h * d:(h + 1) * d], (((1,), (1,)), ((), ())),
                              preferred_element_type=F32) * scale for h, half in streams]
        keeps = [(j * tk + half * tq + c < qi * tq + r) if diagonal else None for _, half in streams]
        css = [cumsum(z, keep) for z, keep in zip(zs, keeps)]
        for (h, half), z, cs, keep in zip(streams, zs, css, keeps):
            a = jnp.exp(z - cs - cins[h])
            if keep is not None:
                a = jnp.where(keep, a, 0.0)
            accs[h] = accs[h] + jnp.dot(a.astype(BF16), v_ref[0, rows(j, half), h * d:(h + 1) * d],
                                        preferred_element_type=F32)
            cins[h] = cins[h] + cs[:, :1]
        return tuple(accs), tuple(cins)

    nfull = qi // 2
    carry = (tuple(jnp.zeros((tq, d), F32) for _ in range(nh)), tuple(jnp.zeros((tq, 1), F32) for _ in range(nh)))
    carry = block(nfull, carry, True)
    accs, _ = lax.fori_loop(0, nfull, lambda i, cr: block(nfull - 1 - i, cr, False), carry)
    for h in range(nh):
        o_ref[0, :, h * d:(h + 1) * d] = accs[h].astype(o_ref.dtype)


def sb_attention(p, n_heads, *, tq=256, nh=2):
    bsz, t_len, _ = p.shape
    d = SB_HEAD_DIM
    assert t_len % (2 * tq) == 0 and n_heads % nh == 0
    ng = n_heads // nh
    w = nh * d
    return pl.pallas_call(
        functools.partial(_sb_kernel, tq=tq, scale=d ** -0.5, nh=nh),
        out_shape=jax.ShapeDtypeStruct((bsz, t_len, n_heads * d), BF16),
        grid=(bsz, ng, t_len // tq),
        in_specs=[pl.BlockSpec((1, tq, w), lambda b, h, i: (b, i, h)),
                  pl.BlockSpec((1, t_len, w), lambda b, h, i: (b, 0, ng + h)),
                  pl.BlockSpec((1, t_len, w), lambda b, h, i: (b, 0, 2 * ng + h))],
        out_specs=pl.BlockSpec((1, tq, w), lambda b, h, i: (b, i, h)),
        compiler_params=_params("parallel", "parallel", "arbitrary"),
        name="sb_attention",
    )(p, p, p)


def _tri_inv_kernel(l_ref, t_ref, *, c):
    nb = c // SUBLANES
    sub = lax.broadcasted_iota(jnp.int32, (SUBLANES, LANES), 0)
    zero = jnp.zeros((SUBLANES, LANES), F32)
    for t in range(c):
        tb = t // SUBLANES
        accs = [zero] * (tb + 1)
        accs[tb] = jnp.where(sub == (t % SUBLANES), 1.0, 0.0)
        for j in range(t):
            lt = jnp.broadcast_to(l_ref[t, j:j + 1, :], (SUBLANES, LANES))
            for cb in range(j // SUBLANES + 1):
                accs[cb] = accs[cb] - lt * t_ref[j, cb * SUBLANES:(cb + 1) * SUBLANES, :]
        for cb in range(nb):
            t_ref[t, cb * SUBLANES:(cb + 1) * SUBLANES, :] = accs[cb] if cb <= tb else zero


def tri_inverse(l):
    shape = l.shape
    c = shape[-1]
    n_sys = math.prod(shape[:-2])
    n = -(-n_sys // LANES) * LANES
    lt = jnp.pad(l.reshape(n_sys, c * c), ((0, n - n_sys), (0, 0))).T.reshape(c, c, n)
    blk = pl.BlockSpec((c, c, LANES), lambda i: (0, 0, i))
    tt = pl.pallas_call(
        functools.partial(_tri_inv_kernel, c=c),
        out_shape=jax.ShapeDtypeStruct((c, c, n), F32),
        grid=(n // LANES,),
        in_specs=[blk],
        out_specs=blk,
        compiler_params=_params("parallel"),
        name="tri_inverse",
    )(lt)
    return tt.reshape(c * c, n).T[:n_sys].reshape(shape)


def _bdot(a, b):
    return jnp.dot(a, b, preferred_element_type=F32)


def _rwkv_decayed(lw, kk, a):
    c = lw.shape[0]
    g = _dot_exact_lhs(_tri(c, "ge").astype(BF16), lw)
    b = a * kk
    return g, b, kk * jnp.exp(g - lw), b * jnp.exp(-g)


def _rwkv_l_kernel(lw_ref, kk_ref, a_ref, l_ref):
    c = lw_ref.shape[1]
    nh = l_ref.shape[2]
    n = RW_HEAD_DIM
    _, _, kq, bk = _rwkv_decayed(lw_ref[0], kk_ref[0], a_ref[0])
    strict = _tri(c, "gt")
    lane = lax.broadcasted_iota(jnp.int32, (c, LANES), 1)
    sel = [lane < n, lane >= n]
    kqb, bkb = kq.astype(BF16), bk.astype(BF16)
    lhs = [jnp.where(sel[h % 2], kqb[:, (h // 2) * LANES:(h // 2 + 1) * LANES], 0) for h in range(nh)]
    prods = [_dot_nt(lhs[h], bkb[:, (h // 2) * LANES:(h // 2 + 1) * LANES]) for h in range(nh)]
    for h in range(nh):
        l_ref[0, 0, h] = jnp.where(strict, prods[h], 0.0)


def _rwkv_local_kernel(r_ref, lw_ref, k_ref, v_ref, kk_ref, a_ref, t_ref,
                       qeff_ref, yl_ref, mc_ref, n_ref, pc_ref):
    c = lw_ref.shape[1]
    nh = t_ref.shape[2]
    n = RW_HEAD_DIM
    npair = nh // 2
    r, lw, k, v, kk, a = (ref[0] for ref in (r_ref, lw_ref, k_ref, v_ref, kk_ref, a_ref))
    g, b, kq, bk = _rwkv_decayed(lw, kk, a)
    eg = jnp.exp(g)
    rq = r * eg
    kh = k * jnp.exp(-g)
    glast = g[c - 1:c, :]
    tail = jnp.exp(glast - g)
    kd = (k * tail).astype(BF16)
    bd = (b * tail).astype(BF16)
    pc_ref[0, 0] = jnp.exp(glast)
    kqb, rqb, khb, bkb, vb = (x.astype(BF16) for x in (kq, rq, kh, bk, v))

    lane = lax.broadcasted_iota(jnp.int32, (c, LANES), 1)
    row = lax.broadcasted_iota(jnp.int32, (c, LANES), 0)
    lo_half = lane < n
    sel = [lo_half, jnp.logical_not(lo_half)]
    col = jnp.where(lo_half, lane, lane - n)
    m_uk = jnp.logical_and(lo_half, row > col)
    m_y = row >= col
    m_yb = jnp.logical_and(m_y, jnp.logical_not(lo_half))
    zeros = jnp.zeros((c, LANES), BF16)
    pl_ = lambda x, p: x[:, p * LANES:(p + 1) * LANES]

    heads = range(nh)
    lhs = [jnp.concatenate([jnp.where(sel[h % 2], pl_(kqb, h // 2), 0), jnp.where(sel[h % 2], pl_(rqb, h // 2), 0)],
                           axis=0) for h in heads]
    rhs = [jnp.concatenate([pl_(khb, p), pl_(bkb, p)], axis=0) for p in range(npair)]
    gm = [_dot_nt(lhs[h], rhs[h // 2]) for h in heads]
    a_u = [jnp.where(m_uk, gm[h][:c], 0.0).astype(BF16) for h in heads]
    a_y = [jnp.where(m_y, gm[h][c:], 0.0) for h in heads]
    a_ys = [jnp.where(lo_half, a_y[h], -a_y[h]).astype(BF16) for h in heads]
    a_yb = [jnp.where(m_yb, gm[h][c:], 0.0).astype(BF16) for h in heads]
    vz = [jnp.concatenate([pl_(vb, p), zeros], axis=0) for p in range(npair)]
    x1 = [_bdot(a_u[h], vz[h // 2]).astype(BF16) for h in heads]
    tb = [t_ref[0, 0, h].astype(BF16) for h in heads]
    tw = [_bdot(tb[h], jnp.concatenate([pl_(kqb, h // 2), x1[h]], axis=1)).astype(BF16) for h in heads]
    qe = [pl_(rq, h // 2) - _bdot(a_yb[h], jnp.concatenate([zeros, tw[h][:, :LANES]], axis=0)) for h in heads]
    yl = [_bdot(a_ys[h], jnp.concatenate([pl_(vb, h // 2), tw[h][:, LANES:]], axis=0)) for h in heads]
    r2 = lax.broadcasted_iota(jnp.int32, (LANES, LANES), 0)
    c2 = lax.broadcasted_iota(jnp.int32, (LANES, LANES), 1)
    bdiag = (r2 < n) == (c2 < n)
    for p in range(npair):
        merge = lambda xs: jnp.where(lo_half, xs[2 * p], xs[2 * p + 1])
        wq = merge([t[:, :LANES] for t in tw])
        ut = merge([t[:, LANES:] for t in tw])
        qeff_ref[0, :, p * LANES:(p + 1) * LANES] = merge(qe).astype(qeff_ref.dtype)
        yl_ref[0, :, p * LANES:(p + 1) * LANES] = merge(yl).astype(yl_ref.dtype)
        mc_ref[0, 0, p] = jnp.where(bdiag, _dot_tn(wq, pl_(bd, p)), 0.0).astype(mc_ref.dtype)
        n_ref[0, 0, p] = jnp.where(bdiag, _dot_tn(pl_(vb, p), pl_(kd, p)) - _dot_tn(ut, pl_(bd, p)), 0.0)


def _rwkv_scan_kernel(qeff_ref, yl_ref, mc_ref, n_ref, pc_ref, y_ref, s_ref):
    npair = mc_ref.shape[2]

    @pl.when(pl.program_id(1) == 0)
    def _():
        s_ref[...] = jnp.zeros_like(s_ref)

    pairs = range(npair)
    s = [s_ref[p] for p in pairs]
    sb = [x.astype(BF16) for x in s]
    y = [_dot_nt(qeff_ref[0, :, p * LANES:(p + 1) * LANES], sb[p]) for p in pairs]
    sm = [_bdot(sb[p], mc_ref[0, 0, p]) for p in pairs]
    for p in pairs:
        y_ref[0, :, p * LANES:(p + 1) * LANES] = y[p] + yl_ref[0, :, p * LANES:(p + 1) * LANES].astype(F32)
        s_ref[p] = s[p] * pc_ref[0, 0, :, p * LANES:(p + 1) * LANES] - sm[p] + n_ref[0, 0, p]


def rwkv7_recurrence(r, lw, k, v, kk, a):
    bsz, t_len, width = r.shape
    nh = width // RW_HEAD_DIM
    npair = nh // 2
    c = min(CHUNK, t_len)
    nc = t_len // c
    seq = pl.BlockSpec((1, c, width), lambda b, i: (b, i, 0))
    mat = pl.BlockSpec((1, 1, nh, c, c), lambda b, i: (b, i, 0, 0, 0))
    sq = pl.BlockSpec((1, 1, npair, LANES, LANES), lambda b, i: (b, i, 0, 0, 0))
    vec = pl.BlockSpec((1, 1, 1, width), lambda b, i: (b, i, 0, 0))
    l = pl.pallas_call(
        _rwkv_l_kernel,
        out_shape=jax.ShapeDtypeStruct((bsz, nc, nh, c, c), F32),
        grid=(bsz, nc),
        in_specs=[seq, seq, seq],
        out_specs=mat,
        compiler_params=_params("parallel", "parallel"),
        name="rwkv_l",
    )(lw, kk, a)
    tinv = tri_inverse(l)
    qeff, yl, mc, nn, pc = pl.pallas_call(
        _rwkv_local_kernel,
        out_shape=(jax.ShapeDtypeStruct((bsz, t_len, width), BF16),
                   jax.ShapeDtypeStruct((bsz, t_len, width), BF16),
                   jax.ShapeDtypeStruct((bsz, nc, npair, LANES, LANES), BF16),
                   jax.ShapeDtypeStruct((bsz, nc, npair, LANES, LANES), F32),
                   jax.ShapeDtypeStruct((bsz, nc, 1, width), F32)),
        grid=(bsz, nc),
        in_specs=[seq] * 6 + [mat],
        out_specs=(seq, seq, sq, sq, vec),
        compiler_params=_params("parallel", "parallel"),
        name="rwkv_local",
    )(r, lw, k, v, kk, a, tinv)
    return pl.pallas_call(
        _rwkv_scan_kernel,
        out_shape=jax.ShapeDtypeStruct((bsz, t_len, width), F32),
        grid=(bsz, nc),
        in_specs=[seq, seq, sq, sq, vec],
        out_specs=seq,
        scratch_shapes=[pltpu.VMEM((npair, LANES, LANES), F32)],
        compiler_params=_params("parallel", "arbitrary"),
        name="rwkv_scan",
    )(qeff, yl, mc, nn, pc)


def _gdn_l_kernel(k_ref, beta_ref, gcol_ref, grow_ref, l_ref, *, hv, rep):
    c = k_ref.shape[1]
    dh = GDN_HEAD_DIM
    incl = _tri(c, "ge")
    strict = _tri(c, "gt")
    gc_col = _dot_exact_lhs(incl.astype(BF16), gcol_ref[0, 0, 0])
    gc_row = _dot_exact_rhs(grow_ref[0, 0, 0], _tri(c, "le").astype(BF16))
    beta_all = beta_ref[0, 0, 0]
    ks = [k_ref[0, :, i * dh:(i + 1) * dh] for i in range(hv // rep)]
    kkt = [_dot_nt(x, x) for x in ks]
    decay = [jnp.exp(jnp.minimum(gc_col[:, j:j + 1] - gc_row[j:j + 1, :], 0.0)) for j in range(hv)]
    for j in range(hv):
        l_ref[0, 0, j] = jnp.where(strict, kkt[j // rep] * beta_all[:, j:j + 1] * decay[j], 0.0)


def _gdn_local_kernel(q_ref, k_ref, v_ref, beta_ref, gcol_ref, grow_ref, t_ref,
                      qeff_ref, ol_ref, mc_ref, n_ref, dec_ref, *, hv, rep):
    c = k_ref.shape[1]
    dh = GDN_HEAD_DIM
    incl = _tri(c, "ge")
    gc_col = _dot_exact_lhs(incl.astype(BF16), gcol_ref[0, 0, 0])
    gc_row = _dot_exact_rhs(grow_ref[0, 0, 0], _tri(c, "le").astype(BF16))
    beta_all = beta_ref[0, 0, 0]
    hk = hv // rep
    ks = [k_ref[0, :, i * dh:(i + 1) * dh] for i in range(hk)]
    qs = [q_ref[0, :, i * dh:(i + 1) * dh] for i in range(hk)]
    qk = [_dot_nt(qs[i], ks[i]) for i in range(hk)]
    heads = range(hv)
    gcc = [gc_col[:, j:j + 1] for j in heads]
    decay = [jnp.where(incl, jnp.exp(jnp.minimum(gcc[j] - gc_row[j:j + 1, :], 0.0)), 0.0) for j in heads]
    egc = [jnp.exp(gcc[j]) for j in heads]
    glast = [gcc[j][c - 1:c, :] for j in heads]
    beta = [beta_all[:, j:j + 1] for j in heads]
    attn = [(qk[j // rep] * decay[j]).astype(BF16) for j in heads]
    kd = [(ks[j // rep] * jnp.exp(glast[j] - gcc[j])).astype(BF16) for j in heads]
    rhs = [jnp.concatenate([v_ref[0, :, j * dh:(j + 1) * dh] * beta[j],
                            ks[j // rep] * (beta[j] * egc[j])], axis=1).astype(BF16) for j in heads]
    uw = [jnp.dot(t_ref[0, 0, j].astype(BF16), rhs[j], preferred_element_type=F32).astype(BF16) for j in heads]
    for j in heads:
        u, w = uw[j][:, :dh], uw[j][:, dh:]
        qeff_ref[0, :, j * dh:(j + 1) * dh] = (qs[j // rep] * egc[j]
                                                - jnp.dot(attn[j], w, preferred_element_type=F32)).astype(qeff_ref.dtype)
        ol_ref[0, :, j * dh:(j + 1) * dh] = jnp.dot(attn[j], u, preferred_element_type=F32).astype(ol_ref.dtype)
        mc_ref[0, 0, j] = _dot_tn(kd[j], w).astype(mc_ref.dtype)
        n_ref[0, 0, j] = _dot_tn(kd[j], u).astype(n_ref.dtype)
        dec_ref[0, 0, j] = jnp.broadcast_to(jnp.exp(glast[j]), (1, dh))


def _gdn_scan_kernel(qeff_ref, ol_ref, mc_ref, n_ref, dec_ref, o_ref, s_ref, *, hv):
    dh = GDN_HEAD_DIM

    @pl.when(pl.program_id(2) == 0)
    def _():
        s_ref[...] = jnp.zeros_like(s_ref)

    heads = range(hv)
    s = [s_ref[j] for j in heads]
    sb = [x.astype(BF16) for x in s]
    o = [jnp.dot(qeff_ref[0, :, j * dh:(j + 1) * dh], sb[j], preferred_element_type=F32) for j in heads]
    ms = [jnp.dot(mc_ref[0, 0, j], sb[j], preferred_element_type=F32) for j in heads]
    for j in heads:
        o_ref[0, :, j * dh:(j + 1) * dh] = o[j] + ol_ref[0, :, j * dh:(j + 1) * dh].astype(F32)
        s_ref[j] = s[j] * dec_ref[0, 0, j] - ms[j] + n_ref[0, 0, j].astype(F32)


def gdn_recurrence(q, k, v, g, beta, *, hv_group=8):
    bsz, t_len, kw = q.shape
    dh = GDN_HEAD_DIM
    hk = kw // dh
    hv = v.shape[-1] // dh
    rep = hv // hk
    hvg = min(hv_group, hv)
    ng = hv // hvg
    c = min(CHUNK, t_len)
    nc = t_len // c
    col = lambda x: x.reshape(bsz, nc, c, ng, hvg).transpose(0, 3, 1, 2, 4)
    g_col, beta_col = col(g), col(beta)
    g_row = g_col.transpose(0, 1, 2, 4, 3)
    qk_spec = pl.BlockSpec((1, c, hvg // rep * dh), lambda b, h, i: (b, i, h))
    v_spec = pl.BlockSpec((1, c, hvg * dh), lambda b, h, i: (b, i, h))
    col_spec = pl.BlockSpec((1, 1, 1, c, hvg), lambda b, h, i: (b, h, i, 0, 0))
    row_spec = pl.BlockSpec((1, 1, 1, hvg, c), lambda b, h, i: (b, h, i, 0, 0))
    mat = pl.BlockSpec((1, 1, hvg, c, c), lambda b, h, i: (b, i, h, 0, 0))
    sq = pl.BlockSpec((1, 1, hvg, dh, dh), lambda b, h, i: (b, i, h, 0, 0))
    vec = pl.BlockSpec((1, 1, hvg, 1, dh), lambda b, h, i: (b, i, h, 0, 0))
    l = pl.pallas_call(
        functools.partial(_gdn_l_kernel, hv=hvg, rep=rep),
        out_shape=jax.ShapeDtypeStruct((bsz, nc, hv, c, c), F32),
        grid=(bsz, ng, nc),
        in_specs=[qk_spec, col_spec, col_spec, row_spec],
        out_specs=mat,
        compiler_params=_params("parallel", "parallel", "parallel"),
        name="gdn_l",
    )(k, beta_col, g_col, g_row)
    tinv = tri_inverse(l)
    qeff, ol, mc, n, dec = pl.pallas_call(
        functools.partial(_gdn_local_kernel, hv=hvg, rep=rep),
        out_shape=(jax.ShapeDtypeStruct((bsz, t_len, hv * dh), BF16),
                   jax.ShapeDtypeStruct((bsz, t_len, hv * dh), BF16),
                   jax.ShapeDtypeStruct((bsz, nc, hv, dh, dh), BF16),
                   jax.ShapeDtypeStruct((bsz, nc, hv, dh, dh), F32),
                   jax.ShapeDtypeStruct((bsz, nc, hv, 1, dh), F32)),
        grid=(bsz, ng, nc),
        in_specs=[qk_spec, qk_spec, v_spec, col_spec, col_spec, row_spec, mat],
        out_specs=(v_spec, v_spec, sq, sq, vec),
        compiler_params=_params("parallel", "parallel", "parallel"),
        name="gdn_local",
    )(q, k, v, beta_col, g_col, g_row, tinv)
    return pl.pallas_call(
        functools.partial(_gdn_scan_kernel, hv=hvg),
        out_shape=jax.ShapeDtypeStruct((bsz, t_len, hv * dh), F32),
        grid=(bsz, ng, nc),
        in_specs=[v_spec, v_spec, sq, sq, vec],
        out_specs=v_spec,
        scratch_shapes=[pltpu.VMEM((hvg, dh, dh), F32)],
        compiler_params=_params("parallel", "parallel", "arbitrary"),
        name="gdn_scan",
    )(qeff, ol, mc, n, dec)


def _heads(t, hd):
    return t.reshape(t.shape[:-1] + (t.shape[-1] // hd, hd))


def _l2n(x, eps):
    return x * lax.rsqrt(jnp.sum(x * x, axis=-1, keepdims=True) + eps)


def rwkv7_group(pb, mu, w0, w_up, a0, a_up, g_up, k_k, k_a, r_k, lnx_g, lnx_b, v_first, v0, v_up, sizes):
    bsz, t_len, _ = pb.shape
    prev = jnp.pad(pb, ((0, 0), (1, 0), (0, 0)))[:, :-1]
    xs = pb + (prev - pb) * mu
    offs = [0]
    for s in sizes:
        offs.append(offs[-1] + s)
    parts = [xs[..., offs[i]:offs[i + 1]] for i in range(len(sizes))]
    r, k, v, w_lo, a_lo, g_lo = parts[:6]
    if v_first is None:
        v_first = v
    else:
        v = v + (v_first - v) * jax.nn.sigmoid(v0 + parts[6] @ v_up)
    w_log = -jax.nn.softplus(-(w0 + jnp.tanh(w_lo) @ w_up)) - 0.5
    lw = -jnp.exp(w_log)
    a = jax.nn.sigmoid(a0 + a_lo @ a_up)
    g = jax.nn.sigmoid(g_lo) @ g_up
    kk = _l2n(_heads(k * k_k, RW_HEAD_DIM), RW_KK_EPS).reshape(k.shape)
    k = k * (1.0 + (a - 1.0) * k_a)
    rh, kh, vh = _heads(r, RW_HEAD_DIM), _heads(k, RW_HEAD_DIM), _heads(v, RW_HEAD_DIM)
    y = _heads(rwkv7_recurrence(r, lw, k, v, kk, a), RW_HEAD_DIM)
    mu_y = jnp.mean(y, axis=-1, keepdims=True)
    var_y = jnp.mean(jnp.square(y - mu_y), axis=-1, keepdims=True)
    y = ((y - mu_y) * lax.rsqrt(var_y + RW_LNX_EPS)).reshape(bsz, t_len, -1)
    y = y * lnx_g + lnx_b
    bonus = jnp.sum(rh * kh * r_k, axis=-1, keepdims=True) * vh
    return (y + bonus.reshape(y.shape)) * g, v_first


def gated_deltanet(qkv, z, ba, conv_w, a_log, dt_bias, norm_w, key_width):
    bsz, t_len, _ = qkv.shape
    hv = a_log.shape[0]
    b, a = ba[..., :hv], ba[..., hv:2 * hv]
    pad = jnp.pad(qkv, ((0, 0), (GDN_CONV - 1, 0), (0, 0)))
    conv = sum(conv_w[i] * pad[:, i:i + t_len] for i in range(GDN_CONV))
    qkv = jax.nn.silu(conv)
    q, k, v = qkv[..., :key_width], qkv[..., key_width:2 * key_width], qkv[..., 2 * key_width:]
    q = _l2n(_heads(q, GDN_HEAD_DIM), GDN_QK_EPS).reshape(bsz, t_len, key_width) * (GDN_HEAD_DIM ** -0.5)
    k = _l2n(_heads(k, GDN_HEAD_DIM), GDN_QK_EPS).reshape(bsz, t_len, key_width)
    beta = jax.nn.sigmoid(b)
    g = -jnp.exp(a_log) * jax.nn.softplus(a + dt_bias)
    o = _heads(gdn_recurrence(q, k, v, g, beta), GDN_HEAD_DIM)
    o = o * lax.rsqrt(jnp.mean(o * o, axis=-1, keepdims=True) + GDN_NORM_EPS) * norm_w
    o = o * jax.nn.silu(_heads(z, GDN_HEAD_DIM))
    return o.reshape(bsz, t_len, -1)


def _pad_cols(w, mult):
    n = w.shape[1]
    return jnp.pad(w, ((0, 0), (0, (-n) % mult)))


def kernel(x, ev_w_in, ev_shift, ev_w0, ev_w_up, ev_a0, ev_a_up, ev_g_up, ev_k_k, ev_k_a, ev_r_k, ev_lnx_g, ev_lnx_b, vres_w_down, vres_shift, vres_v0, vres_v_up, ev_w_out, od_w_in, od_conv, od_a_log, od_dt_bias, od_norm_w, od_w_out, ln1_g, ln1_b, mlp_w1, mlp_w2, ln2_g, ln2_b):
    bsz, t_len, d_model = x.shape
    depth = ln1_g.shape[0]
    alpha = (2 * depth) ** 0.25
    m = bsz * t_len
    rw_width = ev_w0.shape[1]
    sb_width = ev_w_out.shape[1] - rw_width
    sb_heads = sb_width // SB_HEAD_DIM
    sb_cols = 3 * sb_width
    lora = [ev_w_up.shape[1], ev_a_up.shape[1], ev_g_up.shape[1]]
    gdn_hv = od_a_log.shape[1]
    val_width = gdn_hv * GDN_HEAD_DIM
    conv_ch = od_conv.shape[2]
    key_width = (conv_ch - val_width) // 2

    x = x.reshape(m, d_model)
    xb = x.astype(BF16)
    v_first = None
    for layer in range(depth):
        if layer % 2 == 0:
            e = layer // 2
            w_sb = ev_w_in[e][:, :sb_cols]
            w_rw = ev_w_in[e][:, sb_cols:]
            mu = ev_shift[e]
            sizes = [rw_width] * 3 + lora
            v0 = v_up = None
            if e > 0:
                w_rw = jnp.concatenate([w_rw, vres_w_down[e - 1]], axis=1)
                mu = jnp.concatenate([mu, vres_shift[e - 1]])
                sizes = sizes + [vres_w_down.shape[2]]
                v0, v_up = vres_v0[e - 1], vres_v_up[e - 1]
            n_rw = w_rw.shape[1]
            p_sb = matmul(xb, w_sb.astype(BF16), out_dtype=BF16, name="proj_sb")
            p_rw = matmul(xb, _pad_cols(w_rw, 512).astype(BF16), name="proj_rw")[:, :n_rw]
            o_sb = sb_attention(p_sb.reshape(bsz, t_len, sb_cols), sb_heads)
            o_rw, v_first = rwkv7_group(p_rw.reshape(bsz, t_len, n_rw), mu, ev_w0[e], ev_w_up[e], ev_a0[e],
                                        ev_a_up[e], ev_g_up[e], ev_k_k[e], ev_k_a[e], ev_r_k[e],
                                        ev_lnx_g[e], ev_lnx_b[e], v_first, v0, v_up, sizes)
            mixed = jnp.concatenate([o_sb, o_rw.astype(BF16)], axis=-1).reshape(m, -1)
            h = matmul(mixed, ev_w_out[e].astype(BF16), name="proj_out")
        else:
            o = layer // 2
            w_main = od_w_in[o][:, :conv_ch + val_width]
            w_ba = od_w_in[o][:, conv_ch + val_width:]
            p_main = matmul(xb, w_main.astype(BF16), name="proj_gdn")
            p_ba = matmul(xb, _pad_cols(w_ba, LANES).astype(BF16), name="proj_gdn_gates")[:, :2 * gdn_hv]
            p_main = p_main.reshape(bsz, t_len, -1)
            mixed = gated_deltanet(p_main[..., :conv_ch], p_main[..., conv_ch:],
                                   p_ba.reshape(bsz, t_len, -1), od_conv[o], od_a_log[o],
                                   od_dt_bias[o], od_norm_w[o], key_width)
            h = matmul(mixed.astype(BF16).reshape(m, -1), od_w_out[o].astype(BF16), name="proj_out")
        x, xb = add_ln(x, h, ln1_g[layer], ln1_b[layer], alpha)
        hid = matmul(xb, mlp_w1[layer].astype(BF16), act="relu2", out_dtype=BF16, name="mlp_up")
        h = matmul(hid, mlp_w2[layer].astype(BF16), name="mlp_down")
        x, xb = add_ln(x, h, ln2_g[layer], ln2_b[layer], alpha)
    return x.reshape(bsz, t_len, d_model)
```

```python
import functools
import math

import jax
import jax.numpy as jnp
from jax import lax
from jax.experimental import pallas as pl
from jax.experimental.pallas import tpu as pltpu

F32 = jnp.float32
BF16 = jnp.bfloat16

LANES = 128
SUBLANES = 8
CHUNK = 64
VMEM_LIMIT = 48 * 1024 * 1024

SB_HEAD_DIM = 128
RW_HEAD_DIM = 64
GDN_HEAD_DIM = 128
GDN_CONV = 4
RW_LNX_EPS = 64e-5
RW_KK_EPS = 1e-12
GDN_NORM_EPS = 1e-6
GDN_QK_EPS = 1e-6
LN_EPS = 1e-5


def _params(*sem):
    return pltpu.CompilerParams(dimension_semantics=sem, vmem_limit_bytes=VMEM_LIMIT)


def _dot(a, b):
    return jnp.dot(a.astype(BF16), b.astype(BF16), preferred_element_type=F32)


def _dot_nt(a, b):
    return lax.dot_general(a.astype(BF16), b.astype(BF16), (((1,), (1,)), ((), ())),
                           preferred_element_type=F32)


def _dot_tn(a, b):
    return lax.dot_general(a.astype(BF16), b.astype(BF16), (((0,), (0,)), ((), ())),
                           preferred_element_type=F32)


def _split3(x):
    x1 = x.astype(BF16)
    r1 = x - x1.astype(F32)
    x2 = r1.astype(BF16)
    x3 = (r1 - x2.astype(F32)).astype(BF16)
    return x1, x2, x3


def _dot_exact_lhs(m01, x):
    x1, x2, x3 = _split3(x)
    d = lambda t: jnp.dot(m01, t, preferred_element_type=F32)
    return d(x1) + d(x2) + d(x3)


def _dot_exact_rhs(x, m01):
    x1, x2, x3 = _split3(x)
    d = lambda t: jnp.dot(t, m01, preferred_element_type=F32)
    return d(x1) + d(x2) + d(x3)


def _tri(n, kind):
    r = lax.broadcasted_iota(jnp.int32, (n, n), 0)
    c = lax.broadcasted_iota(jnp.int32, (n, n), 1)
    return {"ge": r >= c, "gt": r > c, "le": r <= c}[kind]


def _mm_kernel(a_ref, w_ref, o_ref, *scratch, act, nk):
    prod = jnp.dot(a_ref[...], w_ref[...], preferred_element_type=F32)

    def finish(acc):
        if act == "relu2":
            acc = jnp.square(jnp.maximum(acc, 0.0))
        o_ref[...] = acc.astype(o_ref.dtype)

    if nk == 1:
        finish(prod)
    else:
        acc_ref, = scratch
        k = pl.program_id(2)

        @pl.when(k == 0)
        def _():
            acc_ref[...] = prod

        @pl.when(jnp.logical_and(k > 0, k < nk - 1))
        def _():
            acc_ref[...] += prod

        @pl.when(k == nk - 1)
        def _():
            finish(acc_ref[...] + prod)


def _pick(n, cands):
    for c in cands:
        if n % c == 0:
            return c
    raise ValueError(f"no tile for {n}")


def matmul(a, w, *, act=None, out_dtype=F32, name="mm"):
    m, k = a.shape
    k2, n = w.shape
    assert k == k2
    tm = _pick(m, (1024, 512, 256, 128, 64, 32, 16, 8))
    tn = _pick(n, (1024, 512, 256, 128))
    tk = _pick(k, (2048, 1024, 512, 256, 128))
    nk = k // tk
    scratch = [pltpu.VMEM((tm, tn), F32)] if nk > 1 else []
    return pl.pallas_call(
        functools.partial(_mm_kernel, act=act, nk=nk),
        out_shape=jax.ShapeDtypeStruct((m, n), out_dtype),
        grid=(m // tm, n // tn, nk),
        in_specs=[pl.BlockSpec((tm, tk), lambda i, j, kk: (i, kk)),
                  pl.BlockSpec((tk, tn), lambda i, j, kk: (kk, j))],
        out_specs=pl.BlockSpec((tm, tn), lambda i, j, kk: (i, j)),
        scratch_shapes=scratch,
        compiler_params=_params("parallel", "parallel", "arbitrary"),
        name=name,
    )(a, w)


def _mm_concat_kernel(a1_ref, a2_ref, w1_ref, w2_ref, o_ref):
    o_ref[...] = (jnp.dot(a1_ref[...], w1_ref[...], preferred_element_type=F32)
                  + jnp.dot(a2_ref[...], w2_ref[...], preferred_element_type=F32)).astype(o_ref.dtype)


def matmul_concat(a1, a2, w, *, out_dtype=F32, name="mm_concat"):
    m, kh = a1.shape
    assert a2.shape == (m, kh) and w.shape[0] == 2 * kh
    n = w.shape[1]
    tm = _pick(m, (1024, 512, 256, 128, 64, 32, 16, 8))
    tn = _pick(n, (1024, 512, 256, 128))
    lhs = pl.BlockSpec((tm, kh), lambda i, j: (i, 0))
    return pl.pallas_call(
        _mm_concat_kernel,
        out_shape=jax.ShapeDtypeStruct((m, n), out_dtype),
        grid=(m // tm, n // tn),
        in_specs=[lhs, lhs, pl.BlockSpec((kh, tn), lambda i, j: (0, j)), pl.BlockSpec((kh, tn), lambda i, j: (1, j))],
        out_specs=pl.BlockSpec((tm, tn), lambda i, j: (i, j)),
        compiler_params=_params("parallel", "parallel"),
        name=name,
    )(a1, a2, w, w)


def _add_ln_kernel(x_ref, h_ref, g_ref, b_ref, o_ref, ob_ref, *, alpha):
    y = alpha * x_ref[...] + h_ref[...].astype(F32)
    mu = jnp.mean(y, axis=-1, keepdims=True)
    yc = y - mu
    var = jnp.mean(jnp.square(yc), axis=-1, keepdims=True)
    out = yc * lax.rsqrt(var + LN_EPS) * g_ref[...] + b_ref[...]
    o_ref[...] = out
    ob_ref[...] = out.astype(BF16)


def add_ln(x, h, g, b, alpha):
    m, d = x.shape
    tr = _pick(m, (256, 128, 64, 32, 16, 8))
    row = pl.BlockSpec((tr, d), lambda i: (i, 0))
    vec = pl.BlockSpec((1, d), lambda i: (0, 0))
    return pl.pallas_call(
        functools.partial(_add_ln_kernel, alpha=alpha),
        out_shape=(jax.ShapeDtypeStruct((m, d), F32), jax.ShapeDtypeStruct((m, d), BF16)),
        grid=(m // tr,),
        in_specs=[row, row, vec, vec],
        out_specs=(row, row),
        compiler_params=_params("parallel"),
        name="add_ln",
    )(x, h, g.reshape(1, d), b.reshape(1, d))


def _split_hi_lo(x):
    hi = lax.bitcast_convert_type(lax.bitcast_convert_type(x, jnp.uint32) & jnp.uint32(0xFFFF0000), F32)
    return hi.astype(BF16), (x - hi).astype(BF16)


def _sb_kernel(q_ref, k_ref, v_ref, o_ref, *, tq, scale, nh):
    qi = pl.program_id(2)
    d = SB_HEAD_DIM
    tk = 2 * tq
    r = lax.broadcasted_iota(jnp.int32, (tq, tq), 0)
    c = lax.broadcasted_iota(jnp.int32, (tq, tq), 1)
    suffix = (r >= c).astype(BF16)
    suffix2 = jnp.concatenate([suffix, suffix], axis=0)
    qs = [q_ref[0, :, h * d:(h + 1) * d] for h in range(nh)]
    streams = [(h, half) for h in range(nh) for half in (1, 0)]

    def rows(j, half):
        return pl.ds(pl.multiple_of(j * tk + half * tq, tq), tq)

    def cumsum(z, keep):
        sp = jnp.maximum(z, 0.0) + jnp.log(1.0 + jnp.exp(-jnp.abs(z)))
        if keep is not None:
            sp = jnp.where(keep, sp, 0.0)
        hi, lo = _split_hi_lo(sp)
        return jnp.dot(jnp.concatenate([hi, lo], axis=1), suffix2, preferred_element_type=F32)

    def block(j, carry, diagonal):
        accs, cins = list(carry[0]), list(carry[1])
        zs = [lax.dot_general(qs[h], k_ref[0, rows(j, half), h * d:(h + 1) * d], (((1,), (1,)), ((), ())),
                              preferred_element_type=F32) * scale for h, half in streams]
        keeps = [(j * tk + half * tq + c < qi * tq + r) if diagonal else None for _, half in streams]
        css = [cumsum(z, keep) for z, keep in zip(zs, keeps)]
        for (h, half), z, cs, keep in zip(streams, zs, css, keeps):
            a = jnp.exp(z - cs - cins[h])
            if keep is not None:
                a = jnp.where(keep, a, 0.0)
            accs[h] = accs[h] + jnp.dot(a.astype(BF16), v_ref[0, rows(j, half), h * d:(h + 1) * d],
                                        preferred_element_type=F32)
            cins[h] = cins[h] + cs[:, :1]
        return tuple(accs), tuple(cins)

    nfull = qi // 2
    carry = (tuple(jnp.zeros((tq, d), F32) for _ in range(nh)), tuple(jnp.zeros((tq, 1), F32) for _ in range(nh)))
    carry = block(nfull, carry, True)
    accs, _ = lax.fori_loop(0, nfull, lambda i, cr: block(nfull - 1 - i, cr, False), carry)
    for h in range(nh):
        o_ref[0, :, h * d:(h + 1) * d] = accs[h].astype(o_ref.dtype)


def sb_attention(p, n_heads, *, tq=256, nh=2):
    bsz, t_len, _ = p.shape
    d = SB_HEAD_DIM
    assert t_len % (2 * tq) == 0 and n_heads % nh == 0
    ng = n_heads // nh
    w = nh * d
    return pl.pallas_call(
        functools.partial(_sb_kernel, tq=tq, scale=d ** -0.5, nh=nh),
        out_shape=jax.ShapeDtypeStruct((bsz, t_len, n_heads * d), BF16),
        grid=(bsz, ng, t_len // tq),
        in_specs=[pl.BlockSpec((1, tq, w), lambda b, h, i: (b, i, h)),
                  pl.BlockSpec((1, t_len, w), lambda b, h, i: (b, 0, ng + h)),
                  pl.BlockSpec((1, t_len, w), lambda b, h, i: (b, 0, 2 * ng + h))],
        out_specs=pl.BlockSpec((1, tq, w), lambda b, h, i: (b, i, h)),
        compiler_params=_params("parallel", "parallel", "arbitrary"),
        name="sb_attention",
    )(p, p, p)


def _tri_inv_kernel(l_ref, t_ref, *, c):
    nb = c // SUBLANES
    sub = lax.broadcasted_iota(jnp.int32, (SUBLANES, LANES), 0)
    zero = jnp.zeros((SUBLANES, LANES), F32)
    for t in range(c):
        tb = t // SUBLANES
        accs = [zero] * (tb + 1)
        accs[tb] = jnp.where(sub == (t % SUBLANES), 1.0, 0.0)
        for j in range(t):
            lt = jnp.broadcast_to(l_ref[t, j:j + 1, :], (SUBLANES, LANES))
            for cb in range(j // SUBLANES + 1):
                accs[cb] = accs[cb] - lt * t_ref[j, cb * SUBLANES:(cb + 1) * SUBLANES, :]
        for cb in range(nb):
            t_ref[t, cb * SUBLANES:(cb + 1) * SUBLANES, :] = accs[cb] if cb <= tb else zero


def tri_inverse(l):
    shape = l.shape
    c = shape[-1]
    n_sys = math.prod(shape[:-2])
    n = -(-n_sys // LANES) * LANES
    lt = jnp.pad(l.reshape(n_sys, c * c), ((0, n - n_sys), (0, 0))).T.reshape(c, c, n)
    blk = pl.BlockSpec((c, c, LANES), lambda i: (0, 0, i))
    tt = pl.pallas_call(
        functools.partial(_tri_inv_kernel, c=c),
        out_shape=jax.ShapeDtypeStruct((c, c, n), F32),
        grid=(n // LANES,),
        in_specs=[blk],
        out_specs=blk,
        compiler_params=_params("parallel"),
        name="tri_inverse",
    )(lt)
    return tt.reshape(c * c, n).T[:n_sys].reshape(shape)


def _bdot(a, b):
    return jnp.dot(a, b, preferred_element_type=F32)


def _rwkv_decayed(lw, kk, a):
    c = lw.shape[0]
    g = _dot_exact_lhs(_tri(c, "ge").astype(BF16), lw)
    b = a * kk
    return g, b, kk * jnp.exp(g - lw), b * jnp.exp(-g)


def _rwkv_l_kernel(lw_ref, kk_ref, a_ref, l_ref):
    c = lw_ref.shape[1]
    nh = l_ref.shape[2]
    n = RW_HEAD_DIM
    _, _, kq, bk = _rwkv_decayed(lw_ref[0], kk_ref[0], a_ref[0])
    strict = _tri(c, "gt")
    lane = lax.broadcasted_iota(jnp.int32, (c, LANES), 1)
    sel = [lane < n, lane >= n]
    kqb, bkb = kq.astype(BF16), bk.astype(BF16)
    lhs = [jnp.where(sel[h % 2], kqb[:, (h // 2) * LANES:(h // 2 + 1) * LANES], 0) for h in range(nh)]
    prods = [_dot_nt(lhs[h], bkb[:, (h // 2) * LANES:(h // 2 + 1) * LANES]) for h in range(nh)]
    for h in range(nh):
        l_ref[0, 0, h] = jnp.where(strict, prods[h], 0.0)


def _rwkv_local_kernel(r_ref, lw_ref, k_ref, v_ref, kk_ref, a_ref, t_ref,
                       qeff_ref, yl_ref, mc_ref, n_ref, pc_ref):
    c = lw_ref.shape[1]
    nh = t_ref.shape[2]
    n = RW_HEAD_DIM
    npair = nh // 2
    r, lw, k, v, kk, a = (ref[0] for ref in (r_ref, lw_ref, k_ref, v_ref, kk_ref, a_ref))
    g, b, kq, bk = _rwkv_decayed(lw, kk, a)
    eg = jnp.exp(g)
    rq = r * eg
    kh = k * jnp.exp(-g)
    glast = g[c - 1:c, :]
    tail = jnp.exp(glast - g)
    kd = (k * tail).astype(BF16)
    bd = (b * tail).astype(BF16)
    pc_ref[0, 0] = jnp.exp(glast)
    kqb, rqb, khb, bkb, vb = (x.astype(BF16) for x in (kq, rq, kh, bk, v))

    lane = lax.broadcasted_iota(jnp.int32, (c, LANES), 1)
    row = lax.broadcasted_iota(jnp.int32, (c, LANES), 0)
    lo_half = lane < n
    sel = [lo_half, jnp.logical_not(lo_half)]
    col = jnp.where(lo_half, lane, lane - n)
    m_uk = jnp.logical_and(lo_half, row > col)
    m_y = row >= col
    m_yb = jnp.logical_and(m_y, jnp.logical_not(lo_half))
    zeros = jnp.zeros((c, LANES), BF16)
    pl_ = lambda x, p: x[:, p * LANES:(p + 1) * LANES]

    heads = range(nh)
    lhs = [jnp.concatenate([jnp.where(sel[h % 2], pl_(kqb, h // 2), 0), jnp.where(sel[h % 2], pl_(rqb, h // 2), 0)],
                           axis=0) for h in heads]
    rhs = [jnp.concatenate([pl_(khb, p), pl_(bkb, p)], axis=0) for p in range(npair)]
    gm = [_dot_nt(lhs[h], rhs[h // 2]) for h in heads]
    a_u = [jnp.where(m_uk, gm[h][:c], 0.0).astype(BF16) for h in heads]
    a_y = [jnp.where(m_y, gm[h][c:], 0.0) for h in heads]
    a_ys = [jnp.where(lo_half, a_y[h], -a_y[h]).astype(BF16) for h in heads]
    a_yb = [jnp.where(m_yb, gm[h][c:], 0.0).astype(BF16) for h in heads]
    vz = [jnp.concatenate([pl_(vb, p), zeros], axis=0) for p in range(npair)]
    x1 = [_bdot(a_u[h], vz[h // 2]).astype(BF16) for h in heads]
    tb = [t_ref[0, 0, h].astype(BF16) for h in heads]
    tw = [_bdot(tb[h], jnp.concatenate([pl_(kqb, h // 2), x1[h]], axis=1)).astype(BF16) for h in heads]
    qe = [pl_(rq, h // 2) - _bdot(a_yb[h], jnp.concatenate([zeros, tw[h][:, :LANES]], axis=0)) for h in heads]
    yl = [_bdot(a_ys[h], jnp.concatenate([pl_(vb, h // 2), tw[h][:, LANES:]], axis=0)) for h in heads]
    r2 = lax.broadcasted_iota(jnp.int32, (LANES, LANES), 0)
    c2 = lax.broadcasted_iota(jnp.int32, (LANES, LANES), 1)
    bdiag = (r2 < n) == (c2 < n)
    for p in range(npair):
        merge = lambda xs: jnp.where(lo_half, xs[2 * p], xs[2 * p + 1])
        wq = merge([t[:, :LANES] for t in tw])
        ut = merge([t[:, LANES:] for t in tw])
        qeff_ref[0, :, p * LANES:(p + 1) * LANES] = merge(qe).astype(qeff_ref.dtype)
        yl_ref[0, :, p * LANES:(p + 1) * LANES] = merge(yl).astype(yl_ref.dtype)
        mc_ref[0, 0, p] = jnp.where(bdiag, _dot_tn(wq, pl_(bd, p)), 0.0).astype(mc_ref.dtype)
        n_ref[0, 0, p] = jnp.where(bdiag, _dot_tn(pl_(vb, p), pl_(kd, p)) - _dot_tn(ut, pl_(bd, p)),
                                   0.0).astype(n_ref.dtype)


def _rwkv_scan_kernel(qeff_ref, yl_ref, mc_ref, n_ref, pc_ref, y_ref, s_ref):
    npair = mc_ref.shape[2]

    @pl.when(pl.program_id(1) == 0)
    def _():
        s_ref[...] = jnp.zeros_like(s_ref)

    pairs = range(npair)
    s = [s_ref[p] for p in pairs]
    sb = [x.astype(BF16) for x in s]
    y = [_dot_nt(qeff_ref[0, :, p * LANES:(p + 1) * LANES], sb[p]) for p in pairs]
    sm = [_bdot(sb[p], mc_ref[0, 0, p]) for p in pairs]
    for p in pairs:
        y_ref[0, :, p * LANES:(p + 1) * LANES] = y[p] + yl_ref[0, :, p * LANES:(p + 1) * LANES].astype(F32)
        s_ref[p] = s[p] * pc_ref[0, 0, :, p * LANES:(p + 1) * LANES] - sm[p] + n_ref[0, 0, p].astype(F32)


def rwkv7_recurrence(r, lw, k, v, kk, a):
    bsz, t_len, width = r.shape
    nh = width // RW_HEAD_DIM
    npair = nh // 2
    c = min(CHUNK, t_len)
    nc = t_len // c
    seq = pl.BlockSpec((1, c, width), lambda b, i: (b, i, 0))
    mat = pl.BlockSpec((1, 1, nh, c, c), lambda b, i: (b, i, 0, 0, 0))
    sq = pl.BlockSpec((1, 1, npair, LANES, LANES), lambda b, i: (b, i, 0, 0, 0))
    vec = pl.BlockSpec((1, 1, 1, width), lambda b, i: (b, i, 0, 0))
    l = pl.pallas_call(
        _rwkv_l_kernel,
        out_shape=jax.ShapeDtypeStruct((bsz, nc, nh, c, c), F32),
        grid=(bsz, nc),
        in_specs=[seq, seq, seq],
        out_specs=mat,
        compiler_params=_params("parallel", "parallel"),
        name="rwkv_l",
    )(lw, kk, a)
    tinv = tri_inverse(l)
    qeff, yl, mc, nn, pc = pl.pallas_call(
        _rwkv_local_kernel,
        out_shape=(jax.ShapeDtypeStruct((bsz, t_len, width), BF16),
                   jax.ShapeDtypeStruct((bsz, t_len, width), BF16),
                   jax.ShapeDtypeStruct((bsz, nc, npair, LANES, LANES), BF16),
                   jax.ShapeDtypeStruct((bsz, nc, npair, LANES, LANES), BF16),
                   jax.ShapeDtypeStruct((bsz, nc, 1, width), F32)),
        grid=(bsz, nc),
        in_specs=[seq] * 6 + [mat],
        out_specs=(seq, seq, sq, sq, vec),
        compiler_params=_params("parallel", "parallel"),
        name="rwkv_local",
    )(r, lw, k, v, kk, a, tinv)
    return pl.pallas_call(
        _rwkv_scan_kernel,
        out_shape=jax.ShapeDtypeStruct((bsz, t_len, width), F32),
        grid=(bsz, nc),
        in_specs=[seq, seq, sq, sq, vec],
        out_specs=seq,
        scratch_shapes=[pltpu.VMEM((npair, LANES, LANES), F32)],
        compiler_params=_params("parallel", "arbitrary"),
        name="rwkv_scan",
    )(qeff, yl, mc, nn, pc)


def _gdn_l_kernel(k_ref, beta_ref, gcol_ref, grow_ref, l_ref, *, hv, rep):
    c = k_ref.shape[1]
    dh = GDN_HEAD_DIM
    incl = _tri(c, "ge")
    strict = _tri(c, "gt")
    gc_col = _dot_exact_lhs(incl.astype(BF16), gcol_ref[0, 0, 0])
    gc_row = _dot_exact_rhs(grow_ref[0, 0, 0], _tri(c, "le").astype(BF16))
    beta_all = beta_ref[0, 0, 0]
    ks = [k_ref[0, :, i * dh:(i + 1) * dh] for i in range(hv // rep)]
    kkt = [_dot_nt(x, x) for x in ks]
    decay = [jnp.exp(jnp.minimum(gc_col[:, j:j + 1] - gc_row[j:j + 1, :], 0.0)) for j in range(hv)]
    for j in range(hv):
        l_ref[0, 0, j] = jnp.where(strict, kkt[j // rep] * beta_all[:, j:j + 1] * decay[j], 0.0)


def _gdn_local_kernel(q_ref, k_ref, v_ref, beta_ref, gcol_ref, grow_ref, t_ref,
                      qeff_ref, ol_ref, mc_ref, n_ref, dec_ref, *, hv, rep):
    c = k_ref.shape[1]
    dh = GDN_HEAD_DIM
    incl = _tri(c, "ge")
    gc_col = _dot_exact_lhs(incl.astype(BF16), gcol_ref[0, 0, 0])
    gc_row = _dot_exact_rhs(grow_ref[0, 0, 0], _tri(c, "le").astype(BF16))
    beta_all = beta_ref[0, 0, 0]
    hk = hv // rep
    ks = [k_ref[0, :, i * dh:(i + 1) * dh] for i in range(hk)]
    qs = [q_ref[0, :, i * dh:(i + 1) * dh] for i in range(hk)]
    qk = [_dot_nt(qs[i], ks[i]) for i in range(hk)]
    heads = range(hv)
    gcc = [gc_col[:, j:j + 1] for j in heads]
    decay = [jnp.where(incl, jnp.exp(jnp.minimum(gcc[j] - gc_row[j:j + 1, :], 0.0)), 0.0) for j in heads]
    egc = [jnp.exp(gcc[j]) for j in heads]
    glast = [gcc[j][c - 1:c, :] for j in heads]
    beta = [beta_all[:, j:j + 1] for j in heads]
    attn = [(qk[j // rep] * decay[j]).astype(BF16) for j in heads]
    kd = [(ks[j // rep] * jnp.exp(glast[j] - gcc[j])).astype(BF16) for j in heads]
    rhs = [jnp.concatenate([v_ref[0, :, j * dh:(j + 1) * dh] * beta[j],
                            ks[j // rep] * (beta[j] * egc[j])], axis=1).astype(BF16) for j in heads]
    uw = [jnp.dot(t_ref[0, 0, j].astype(BF16), rhs[j], preferred_element_type=F32).astype(BF16) for j in heads]
    for j in heads:
        u, w = uw[j][:, :dh], uw[j][:, dh:]
        qeff_ref[0, :, j * dh:(j + 1) * dh] = (qs[j // rep] * egc[j]
                                                - jnp.dot(attn[j], w, preferred_element_type=F32)).astype(qeff_ref.dtype)
        ol_ref[0, :, j * dh:(j + 1) * dh] = jnp.dot(attn[j], u, preferred_element_type=F32).astype(ol_ref.dtype)
        mc_ref[0, 0, j] = _dot_tn(kd[j], w).astype(mc_ref.dtype)
        n_ref[0, 0, j] = _dot_tn(kd[j], u).astype(n_ref.dtype)
        dec_ref[0, 0, j] = jnp.broadcast_to(jnp.exp(glast[j]), (1, dh))


def _gdn_scan_kernel(qeff_ref, ol_ref, mc_ref, n_ref, dec_ref, z_ref, nw_ref, o_ref, s_ref, *, hv):
    dh = GDN_HEAD_DIM

    @pl.when(pl.program_id(2) == 0)
    def _():
        s_ref[...] = jnp.zeros_like(s_ref)

    heads = range(hv)
    s = [s_ref[j] for j in heads]
    sb = [x.astype(BF16) for x in s]
    o = [jnp.dot(qeff_ref[0, :, j * dh:(j + 1) * dh], sb[j], preferred_element_type=F32) for j in heads]
    ms = [jnp.dot(mc_ref[0, 0, j], sb[j], preferred_element_type=F32) for j in heads]
    for j in heads:
        sl = slice(j * dh, (j + 1) * dh)
        oj = o[j] + ol_ref[0, :, sl].astype(F32)
        oj = oj * lax.rsqrt(jnp.mean(oj * oj, axis=-1, keepdims=True) + GDN_NORM_EPS) * nw_ref[...]
        z = z_ref[0, :, sl]
        o_ref[0, :, sl] = (oj * (z * jax.nn.sigmoid(z))).astype(o_ref.dtype)
        s_ref[j] = s[j] * dec_ref[0, 0, j] - ms[j] + n_ref[0, 0, j].astype(F32)


def gdn_recurrence(q, k, v, g, beta, p_main, z_col0, norm_w, *, hv_group=8):
    bsz, t_len, kw = q.shape
    dh = GDN_HEAD_DIM
    hk = kw // dh
    hv = v.shape[-1] // dh
    rep = hv // hk
    hvg = min(hv_group, hv)
    ng = hv // hvg
    c = min(CHUNK, t_len)
    nc = t_len // c
    col = lambda x: x.reshape(bsz, nc, c, ng, hvg).transpose(0, 3, 1, 2, 4)
    g_col, beta_col = col(g), col(beta)
    g_row = g_col.transpose(0, 1, 2, 4, 3)
    qk_spec = pl.BlockSpec((1, c, hvg // rep * dh), lambda b, h, i: (b, i, h))
    v_spec = pl.BlockSpec((1, c, hvg * dh), lambda b, h, i: (b, i, h))
    col_spec = pl.BlockSpec((1, 1, 1, c, hvg), lambda b, h, i: (b, h, i, 0, 0))
    row_spec = pl.BlockSpec((1, 1, 1, hvg, c), lambda b, h, i: (b, h, i, 0, 0))
    mat = pl.BlockSpec((1, 1, hvg, c, c), lambda b, h, i: (b, i, h, 0, 0))
    sq = pl.BlockSpec((1, 1, hvg, dh, dh), lambda b, h, i: (b, i, h, 0, 0))
    vec = pl.BlockSpec((1, 1, hvg, 1, dh), lambda b, h, i: (b, i, h, 0, 0))
    l = pl.pallas_call(
        functools.partial(_gdn_l_kernel, hv=hvg, rep=rep),
        out_shape=jax.ShapeDtypeStruct((bsz, nc, hv, c, c), F32),
        grid=(bsz, ng, nc),
        in_specs=[qk_spec, col_spec, col_spec, row_spec],
        out_specs=mat,
        compiler_params=_params("parallel", "parallel", "parallel"),
        name="gdn_l",
    )(k, beta_col, g_col, g_row)
    tinv = tri_inverse(l)
    qeff, ol, mc, n, dec = pl.pallas_call(
        functools.partial(_gdn_local_kernel, hv=hvg, rep=rep),
        out_shape=(jax.ShapeDtypeStruct((bsz, t_len, hv * dh), BF16),
                   jax.ShapeDtypeStruct((bsz, t_len, hv * dh), BF16),
                   jax.ShapeDtypeStruct((bsz, nc, hv, dh, dh), BF16),
                   jax.ShapeDtypeStruct((bsz, nc, hv, dh, dh), BF16),
                   jax.ShapeDtypeStruct((bsz, nc, hv, 1, dh), F32)),
        grid=(bsz, ng, nc),
        in_specs=[qk_spec, qk_spec, v_spec, col_spec, col_spec, row_spec, mat],
        out_specs=(v_spec, v_spec, sq, sq, vec),
        compiler_params=_params("parallel", "parallel", "parallel"),
        name="gdn_local",
    )(q, k, v, beta_col, g_col, g_row, tinv)
    z0 = z_col0 // (hvg * dh)
    z_spec = pl.BlockSpec((1, c, hvg * dh), lambda b, h, i: (b, i, z0 + h))
    return pl.pallas_call(
        functools.partial(_gdn_scan_kernel, hv=hvg),
        out_shape=jax.ShapeDtypeStruct((bsz, t_len, hv * dh), BF16),
        grid=(bsz, ng, nc),
        in_specs=[v_spec, v_spec, sq, sq, vec, z_spec, pl.BlockSpec((1, dh), lambda b, h, i: (0, 0))],
        out_specs=v_spec,
        scratch_shapes=[pltpu.VMEM((hvg, dh, dh), F32)],
        compiler_params=_params("parallel", "parallel", "arbitrary"),
        name="gdn_scan",
    )(qeff, ol, mc, n, dec, p_main, norm_w.reshape(1, dh))


def _heads(t, hd):
    return t.reshape(t.shape[:-1] + (t.shape[-1] // hd, hd))


def _l2n(x, eps):
    return x * lax.rsqrt(jnp.sum(x * x, axis=-1, keepdims=True) + eps)


def _softplus(x):
    return jnp.maximum(x, 0.0) + jnp.log(1.0 + jnp.exp(-jnp.abs(x)))


def _head_sum_matrix(n, value):
    r = lax.broadcasted_iota(jnp.int32, (LANES, LANES), 0) // n
    c = lax.broadcasted_iota(jnp.int32, (LANES, LANES), 1) // n
    return jnp.where(r == c, value, 0.0).astype(BF16)


def _rwkv_prep_kernel(*refs, has_vres):
    if has_vres:
        (x_ref, halo_ref, mu_ref, w0_ref, wup_ref, a0_ref, aup_ref, gup_ref, kk_ref, ka_ref,
         vf_ref, v0_ref, vup_ref, r_out, lw_out, k_out, v_out, kkn_out, a_out, g_out) = refs
    else:
        (x_ref, halo_ref, mu_ref, w0_ref, wup_ref, a0_ref, aup_ref, gup_ref, kk_ref, ka_ref,
         r_out, lw_out, k_out, v_out, kkn_out, a_out, g_out) = refs
    x = x_ref[0]
    tr = x.shape[0]
    width = w0_ref.shape[1]
    halo = jnp.where(pl.program_id(1) > 0, halo_ref[0], 0.0)
    prev = jnp.concatenate([halo, x], axis=0)[SUBLANES - 1:SUBLANES - 1 + tr]
    xs = x + (prev - x) * mu_ref[...]
    o = 3 * width
    n_w, n_a, n_g = wup_ref.shape[0], aup_ref.shape[0], gup_ref.shape[0]
    r, k, v = xs[:, :width], xs[:, width:2 * width], xs[:, 2 * width:o]
    w_lo, a_lo, g_lo = xs[:, o:o + n_w], xs[:, o + n_w:o + n_w + n_a], xs[:, o + n_w + n_a:o + n_w + n_a + n_g]
    if has_vres:
        v_lo = xs[:, o + n_w + n_a + n_g:o + n_w + n_a + n_g + vup_ref.shape[0]]
        v = v + (vf_ref[0] - v) * jax.nn.sigmoid(v0_ref[...] + _dot(v_lo, vup_ref[...]))
    w_log = -_softplus(-(w0_ref[...] + _dot(jnp.tanh(w_lo), wup_ref[...]))) - 0.5
    a = jax.nn.sigmoid(a0_ref[...] + _dot(a_lo, aup_ref[...]))
    kkr = k * kk_ref[...]
    sq = kkr * kkr
    ones = _head_sum_matrix(RW_HEAD_DIM, 1.0)
    for p in range(width // LANES):
        sl = slice(p * LANES, (p + 1) * LANES)
        ss = _dot_exact_rhs(sq[:, sl], ones)
        kkn_out[0, :, sl] = kkr[:, sl] * lax.rsqrt(ss + RW_KK_EPS)
    r_out[0] = r
    lw_out[0] = -jnp.exp(w_log)
    k_out[0] = k * (1.0 + (a - 1.0) * ka_ref[...])
    v_out[0] = v
    a_out[0] = a
    g_out[0] = _dot(jax.nn.sigmoid(g_lo), gup_ref[...])


def rwkv_prep(p, mu, w0, w_up, a0, a_up, g_up, k_k, k_a, v_first, v0, v_up, *, tr=256):
    bsz, t_len, wp = p.shape
    width = w0.shape[-1]
    tr = min(tr, t_len)
    has_vres = v_first is not None
    rows = pl.BlockSpec((1, tr, wp), lambda b, i: (b, i, 0))
    halo = pl.BlockSpec((1, SUBLANES, wp), lambda b, i: (b, jnp.maximum(i * (tr // SUBLANES) - 1, 0), 0))
    full = lambda a: pl.BlockSpec(a.shape, lambda b, i: (0,) * a.ndim)
    seq = pl.BlockSpec((1, tr, width), lambda b, i: (b, i, 0))
    vec = lambda a: a.reshape(1, -1)
    args = [p, p, vec(mu), vec(w0), w_up, vec(a0), a_up, g_up, vec(k_k), vec(k_a)]
    specs = [rows, halo] + [full(a) for a in args[2:]]
    if has_vres:
        extra = [v_first, vec(v0), v_up]
        args += extra
        specs += [seq, full(extra[1]), full(extra[2])]
    return pl.pallas_call(
        functools.partial(_rwkv_prep_kernel, has_vres=has_vres),
        out_shape=tuple(jax.ShapeDtypeStruct((bsz, t_len, width), F32) for _ in range(7)),
        grid=(bsz, t_len // tr),
        in_specs=specs,
        out_specs=tuple(seq for _ in range(7)),
        compiler_params=_params("parallel", "parallel"),
        name="rwkv_prep",
    )(*args)


def _rwkv_post_kernel(y_ref, r_ref, k_ref, v_ref, g_ref, rk_ref, lg_ref, lb_ref, o_ref):
    n = RW_HEAD_DIM
    mean_m = _head_sum_matrix(n, 1.0 / n)
    sum_m = _head_sum_matrix(n, 1.0)
    for p in range(y_ref.shape[2] // LANES):
        sl = slice(p * LANES, (p + 1) * LANES)
        y = y_ref[0, :, sl]
        d = y - _dot_exact_rhs(y, mean_m)
        var = _dot_exact_rhs(d * d, mean_m)
        yn = d * lax.rsqrt(var + RW_LNX_EPS) * lg_ref[:, sl] + lb_ref[:, sl]
        bonus = _dot_exact_rhs(r_ref[0, :, sl] * k_ref[0, :, sl] * rk_ref[:, sl], sum_m) * v_ref[0, :, sl]
        o_ref[0, :, sl] = ((yn + bonus) * g_ref[0, :, sl]).astype(o_ref.dtype)


def rwkv_post(y, r, k, v, g, r_k, lnx_g, lnx_b, *, tr=256):
    bsz, t_len, width = y.shape
    tr = min(tr, t_len)
    seq = pl.BlockSpec((1, tr, width), lambda b, i: (b, i, 0))
    vec = pl.BlockSpec((1, width), lambda b, i: (0, 0))
    return pl.pallas_call(
        _rwkv_post_kernel,
        out_shape=jax.ShapeDtypeStruct((bsz, t_len, width), BF16),
        grid=(bsz, t_len // tr),
        in_specs=[seq] * 5 + [vec] * 3,
        out_specs=seq,
        compiler_params=_params("parallel", "parallel"),
        name="rwkv_post",
    )(y, r, k, v, g, r_k.reshape(1, width), lnx_g.reshape(1, width), lnx_b.reshape(1, width))


def _pad_rows(w, mult):
    return jnp.pad(w, ((0, (-w.shape[0]) % mult), (0, 0)))


def rwkv7_group(xb, w_rw, mu, w0, w_up, a0, a_up, g_up, k_k, k_a, r_k, lnx_g, lnx_b, v_first, v0, v_up,
                bsz, t_len):
    width = w0.shape[-1]
    lora = [w_up.shape[0], a_up.shape[0], g_up.shape[0]] + ([v_up.shape[0]] if v_up is not None else [])
    cols, mus, off = [w_rw[:, :3 * width]], [mu[:3 * width]], 3 * width
    for n in lora:
        cols.append(_pad_cols(w_rw[:, off:off + n], LANES))
        mus.append(jnp.pad(mu[off:off + n], (0, (-n) % LANES)))
        off += n
    w_lay = _pad_cols(jnp.concatenate(cols, axis=1), 256)
    mu_lay = jnp.pad(jnp.concatenate(mus), (0, w_lay.shape[1] - sum(m.shape[0] for m in mus)))
    p = matmul(xb, w_lay.astype(BF16), name="proj_rw").reshape(bsz, t_len, -1)
    up = lambda w: _pad_rows(w, LANES).astype(BF16)
    r, lw, k, v, kk, a, g = rwkv_prep(p, mu_lay, w0, up(w_up), a0, up(a_up), up(g_up), k_k, k_a,
                                      v_first, v0, None if v_up is None else up(v_up))
    if v_first is None:
        v_first = v
    y = rwkv7_recurrence(r, lw, k, v, kk, a)
    return rwkv_post(y, r, k, v, g, r_k.reshape(-1), lnx_g, lnx_b), v_first


def _gdn_prep_kernel(x_ref, halo_ref, w_ref, o_ref, *, mode):
    x = x_ref[0]
    tr = x.shape[0]
    taps = w_ref.shape[0]
    halo = jnp.where(pl.program_id(1) > 0, halo_ref[0], 0.0)
    xc = jnp.concatenate([halo, x], axis=0)
    w = w_ref[...]
    y = sum(w[i:i + 1, :] * xc[SUBLANES - taps + 1 + i:SUBLANES - taps + 1 + i + tr] for i in range(taps))
    y = y * jax.nn.sigmoid(y)
    dh = GDN_HEAD_DIM
    for h in range(y.shape[1] // dh):
        seg = y[:, h * dh:(h + 1) * dh]
        if mode != "v":
            seg = seg * lax.rsqrt(jnp.sum(seg * seg, axis=-1, keepdims=True) + GDN_QK_EPS)
        if mode == "q":
            seg = seg * (dh ** -0.5)
        o_ref[0, :, h * dh:(h + 1) * dh] = seg.astype(o_ref.dtype)


def gdn_prep(p, conv_w, col0, ncols, mode, *, tr=512, tw=1024):
    bsz, t_len, _ = p.shape
    tr, tw = min(tr, t_len), min(tw, ncols)
    c0 = col0 // tw
    rows = pl.BlockSpec((1, tr, tw), lambda b, i, j: (b, i, c0 + j))
    halo = pl.BlockSpec((1, SUBLANES, tw), lambda b, i, j: (b, jnp.maximum(i * (tr // SUBLANES) - 1, 0), c0 + j))
    return pl.pallas_call(
        functools.partial(_gdn_prep_kernel, mode=mode),
        out_shape=jax.ShapeDtypeStruct((bsz, t_len, ncols), BF16),
        grid=(bsz, t_len // tr, ncols // tw),
        in_specs=[rows, halo, pl.BlockSpec((conv_w.shape[0], tw), lambda b, i, j: (0, c0 + j))],
        out_specs=pl.BlockSpec((1, tr, tw), lambda b, i, j: (b, i, j)),
        compiler_params=_params("parallel", "parallel", "parallel"),
        name="gdn_prep_" + mode,
    )(p, p, conv_w)


def gated_deltanet(p_main, ba, conv_w, a_log, dt_bias, norm_w, key_width):
    hv = a_log.shape[0]
    conv_ch = conv_w.shape[1]
    q = gdn_prep(p_main, conv_w, 0, key_width, "q")
    k = gdn_prep(p_main, conv_w, key_width, key_width, "k")
    v = gdn_prep(p_main, conv_w, 2 * key_width, conv_ch - 2 * key_width, "v")
    beta = jax.nn.sigmoid(ba[..., :hv])
    g = -jnp.exp(a_log) * jax.nn.softplus(ba[..., hv:2 * hv] + dt_bias)
    return gdn_recurrence(q, k, v, g, beta, p_main, conv_ch, norm_w)


def _pad_cols(w, mult):
    n = w.shape[1]
    return jnp.pad(w, ((0, 0), (0, (-n) % mult)))


def kernel(x, ev_w_in, ev_shift, ev_w0, ev_w_up, ev_a0, ev_a_up, ev_g_up, ev_k_k, ev_k_a, ev_r_k, ev_lnx_g, ev_lnx_b, vres_w_down, vres_shift, vres_v0, vres_v_up, ev_w_out, od_w_in, od_conv, od_a_log, od_dt_bias, od_norm_w, od_w_out, ln1_g, ln1_b, mlp_w1, mlp_w2, ln2_g, ln2_b):
    bsz, t_len, d_model = x.shape
    depth = ln1_g.shape[0]
    alpha = (2 * depth) ** 0.25
    m = bsz * t_len
    rw_width = ev_w0.shape[1]
    sb_width = ev_w_out.shape[1] - rw_width
    sb_heads = sb_width // SB_HEAD_DIM
    sb_cols = 3 * sb_width
    gdn_hv = od_a_log.shape[1]
    val_width = gdn_hv * GDN_HEAD_DIM
    conv_ch = od_conv.shape[2]
    key_width = (conv_ch - val_width) // 2

    x = x.reshape(m, d_model)
    xb = x.astype(BF16)
    v_first = None
    for layer in range(depth):
        if layer % 2 == 0:
            e = layer // 2
            w_sb = ev_w_in[e][:, :sb_cols]
            w_rw = ev_w_in[e][:, sb_cols:]
            mu = ev_shift[e]
            v0 = v_up = None
            if e > 0:
                w_rw = jnp.concatenate([w_rw, vres_w_down[e - 1]], axis=1)
                mu = jnp.concatenate([mu, vres_shift[e - 1]])
                v0, v_up = vres_v0[e - 1], vres_v_up[e - 1]
            p_sb = matmul(xb, w_sb.astype(BF16), out_dtype=BF16, name="proj_sb")
            o_sb = sb_attention(p_sb.reshape(bsz, t_len, sb_cols), sb_heads)
            o_rw, v_first = rwkv7_group(xb, w_rw, mu, ev_w0[e], ev_w_up[e], ev_a0[e], ev_a_up[e], ev_g_up[e],
                                        ev_k_k[e], ev_k_a[e], ev_r_k[e], ev_lnx_g[e], ev_lnx_b[e],
                                        v_first, v0, v_up, bsz, t_len)
            h = matmul_concat(o_sb.reshape(m, -1), o_rw.reshape(m, -1), ev_w_out[e].astype(BF16), name="proj_out")
        else:
            o = layer // 2
            w_main = od_w_in[o][:, :conv_ch + val_width]
            w_ba = od_w_in[o][:, conv_ch + val_width:]
            p_main = matmul(xb, w_main.astype(BF16), name="proj_gdn").reshape(bsz, t_len, -1)
            p_ba = matmul(xb, _pad_cols(w_ba, LANES).astype(BF16), name="proj_gdn_gates")[:, :2 * gdn_hv]
            mixed = gated_deltanet(p_main, p_ba.reshape(bsz, t_len, -1), od_conv[o], od_a_log[o],
                                   od_dt_bias[o], od_norm_w[o], key_width)
            h = matmul(mixed.reshape(m, -1), od_w_out[o].astype(BF16), name="proj_out")
        x, xb = add_ln(x, h, ln1_g[layer], ln1_b[layer], alpha)
        hid = matmul(xb, mlp_w1[layer].astype(BF16), act="relu2", out_dtype=BF16, name="mlp_up")
        h = matmul(hid, mlp_w2[layer].astype(BF16), name="mlp_down")
        x, xb = add_ln(x, h, ln2_g[layer], ln2_b[layer], alpha)
    return x.reshape(bsz, t_len, d_model)
```

```python
import functools
import math

import jax
import jax.numpy as jnp
from jax import lax
from jax.experimental import pallas as pl
from jax.experimental.pallas import tpu as pltpu

F32 = jnp.float32
BF16 = jnp.bfloat16

LANES = 128
SUBLANES = 8
CHUNK = 64
VMEM_LIMIT = 48 * 1024 * 1024

SB_HEAD_DIM = 128
RW_HEAD_DIM = 64
GDN_HEAD_DIM = 128
GDN_CONV = 4
RW_LNX_EPS = 64e-5
RW_KK_EPS = 1e-12
GDN_NORM_EPS = 1e-6
GDN_QK_EPS = 1e-6
LN_EPS = 1e-5


def _params(*sem):
    return pltpu.CompilerParams(dimension_semantics=sem, vmem_limit_bytes=VMEM_LIMIT)


def _dot(a, b):
    return jnp.dot(a.astype(BF16), b.astype(BF16), preferred_element_type=F32)


def _dot_nt(a, b):
    return lax.dot_general(a.astype(BF16), b.astype(BF16), (((1,), (1,)), ((), ())),
                           preferred_element_type=F32)


def _dot_tn(a, b):
    return lax.dot_general(a.astype(BF16), b.astype(BF16), (((0,), (0,)), ((), ())),
                           preferred_element_type=F32)


def _split3(x):
    x1 = x.astype(BF16)
    r1 = x - x1.astype(F32)
    x2 = r1.astype(BF16)
    x3 = (r1 - x2.astype(F32)).astype(BF16)
    return x1, x2, x3


def _dot_exact_lhs(m01, x):
    x1, x2, x3 = _split3(x)
    d = lambda t: jnp.dot(m01, t, preferred_element_type=F32)
    return d(x1) + d(x2) + d(x3)


def _dot_exact_rhs(x, m01):
    x1, x2, x3 = _split3(x)
    d = lambda t: jnp.dot(t, m01, preferred_element_type=F32)
    return d(x1) + d(x2) + d(x3)


def _tri(n, kind):
    r = lax.broadcasted_iota(jnp.int32, (n, n), 0)
    c = lax.broadcasted_iota(jnp.int32, (n, n), 1)
    return {"ge": r >= c, "gt": r > c, "le": r <= c}[kind]


def _mm_kernel(a_ref, w_ref, o_ref, *scratch, act, nk):
    prod = jnp.dot(a_ref[...], w_ref[...], preferred_element_type=F32)

    def finish(acc):
        if act == "relu2":
            acc = jnp.square(jnp.maximum(acc, 0.0))
        o_ref[...] = acc.astype(o_ref.dtype)

    if nk == 1:
        finish(prod)
    else:
        acc_ref, = scratch
        k = pl.program_id(2)

        @pl.when(k == 0)
        def _():
            acc_ref[...] = prod

        @pl.when(jnp.logical_and(k > 0, k < nk - 1))
        def _():
            acc_ref[...] += prod

        @pl.when(k == nk - 1)
        def _():
            finish(acc_ref[...] + prod)


def _pick(n, cands):
    for c in cands:
        if n % c == 0:
            return c
    raise ValueError(f"no tile for {n}")


def matmul(a, w, *, act=None, out_dtype=F32, name="mm"):
    m, k = a.shape
    k2, n = w.shape
    assert k == k2
    tm = _pick(m, (1024, 512, 256, 128, 64, 32, 16, 8))
    tn = _pick(n, (1024, 512, 256, 128))
    tk = _pick(k, (2048, 1024, 512, 256, 128))
    nk = k // tk
    scratch = [pltpu.VMEM((tm, tn), F32)] if nk > 1 else []
    return pl.pallas_call(
        functools.partial(_mm_kernel, act=act, nk=nk),
        out_shape=jax.ShapeDtypeStruct((m, n), out_dtype),
        grid=(m // tm, n // tn, nk),
        in_specs=[pl.BlockSpec((tm, tk), lambda i, j, kk: (i, kk)),
                  pl.BlockSpec((tk, tn), lambda i, j, kk: (kk, j))],
        out_specs=pl.BlockSpec((tm, tn), lambda i, j, kk: (i, j)),
        scratch_shapes=scratch,
        compiler_params=_params("parallel", "parallel", "arbitrary"),
        name=name,
    )(a, w)


def _residual_ln(x, h, g, b, alpha, o_ref, ob_ref):
    y = alpha * x + h
    mu = jnp.mean(y, axis=-1, keepdims=True)
    yc = y - mu
    var = jnp.mean(jnp.square(yc), axis=-1, keepdims=True)
    out = yc * lax.rsqrt(var + LN_EPS) * g + b
    o_ref[...] = out
    ob_ref[...] = out.astype(BF16)


def _mm_ln_kernel(*refs, alpha, n_lhs, nk):
    a_refs, w_refs = refs[:n_lhs], refs[n_lhs:2 * n_lhs]
    x_ref, g_ref, b_ref, o_ref, ob_ref = refs[2 * n_lhs:2 * n_lhs + 5]
    prod = sum(jnp.dot(a[...], w[...], preferred_element_type=F32) for a, w in zip(a_refs, w_refs))
    if nk == 1:
        _residual_ln(x_ref[...], prod, g_ref[...], b_ref[...], alpha, o_ref, ob_ref)
    else:
        acc_ref = refs[-1]
        k = pl.program_id(1)

        @pl.when(k == 0)
        def _():
            acc_ref[...] = prod

        @pl.when(jnp.logical_and(k > 0, k < nk - 1))
        def _():
            acc_ref[...] += prod

        @pl.when(k == nk - 1)
        def _():
            _residual_ln(x_ref[...], acc_ref[...] + prod, g_ref[...], b_ref[...], alpha, o_ref, ob_ref)


def matmul_ln(lhs, w, x, g, b, alpha, *, tm, tk=None, name="mm_ln"):
    m, d = x.shape
    n_lhs = len(lhs)
    kw = lhs[0].shape[1]
    assert all(a.shape == (m, kw) for a in lhs) and w.shape == (n_lhs * kw, d)
    tm = min(tm, m)
    if tk is None:
        nk, tk = 1, kw
        a_specs = [pl.BlockSpec((tm, kw), lambda i, kk: (i, 0)) for _ in lhs]
        w_specs = [pl.BlockSpec((kw, d), lambda i, kk, j=j: (j, 0)) for j in range(n_lhs)]
    else:
        assert n_lhs == 1 and kw % tk == 0
        nk = kw // tk
        a_specs = [pl.BlockSpec((tm, tk), lambda i, kk: (i, kk))]
        w_specs = [pl.BlockSpec((tk, d), lambda i, kk: (kk, 0))]
    row = pl.BlockSpec((tm, d), lambda i, kk: (i, 0))
    vec = pl.BlockSpec((1, d), lambda i, kk: (0, 0))
    return pl.pallas_call(
        functools.partial(_mm_ln_kernel, alpha=alpha, n_lhs=n_lhs, nk=nk),
        out_shape=(jax.ShapeDtypeStruct((m, d), F32), jax.ShapeDtypeStruct((m, d), BF16)),
        grid=(m // tm, nk),
        in_specs=a_specs + w_specs + [row, vec, vec],
        out_specs=(row, row),
        scratch_shapes=[pltpu.VMEM((tm, d), F32)] if nk > 1 else [],
        compiler_params=_params("parallel", "arbitrary"),
        name=name,
    )(*lhs, *([w] * n_lhs), x, g.reshape(1, d), b.reshape(1, d))


def _split_hi_lo(x):
    hi = lax.bitcast_convert_type(lax.bitcast_convert_type(x, jnp.uint32) & jnp.uint32(0xFFFF0000), F32)
    return hi.astype(BF16), (x - hi).astype(BF16)


def _sb_kernel(q_ref, k_ref, v_ref, o_ref, *, tq, scale, nh):
    qi = pl.program_id(2)
    d = SB_HEAD_DIM
    tk = 2 * tq
    r = lax.broadcasted_iota(jnp.int32, (tq, tq), 0)
    c = lax.broadcasted_iota(jnp.int32, (tq, tq), 1)
    suffix = (r >= c).astype(BF16)
    suffix2 = jnp.concatenate([suffix, suffix], axis=0)
    qs = [q_ref[0, :, h * d:(h + 1) * d] for h in range(nh)]
    streams = [(h, half) for h in range(nh) for half in (1, 0)]

    def rows(j, half):
        return pl.ds(pl.multiple_of(j * tk + half * tq, tq), tq)

    def cumsum(z, keep):
        sp = jnp.maximum(z, 0.0) + jnp.log(1.0 + jnp.exp(-jnp.abs(z)))
        if keep is not None:
            sp = jnp.where(keep, sp, 0.0)
        hi, lo = _split_hi_lo(sp)
        return jnp.dot(jnp.concatenate([hi, lo], axis=1), suffix2, preferred_element_type=F32)

    def block(j, carry, diagonal):
        accs, cins = list(carry[0]), list(carry[1])
        zs = [lax.dot_general(qs[h], k_ref[0, rows(j, half), h * d:(h + 1) * d], (((1,), (1,)), ((), ())),
                              preferred_element_type=F32) * scale for h, half in streams]
        keeps = [(j * tk + half * tq + c < qi * tq + r) if diagonal else None for _, half in streams]
        css = [cumsum(z, keep) for z, keep in zip(zs, keeps)]
        for (h, half), z, cs, keep in zip(streams, zs, css, keeps):
            a = jnp.exp(z - cs - cins[h])
            if keep is not None:
                a = jnp.where(keep, a, 0.0)
            accs[h] = accs[h] + jnp.dot(a.astype(BF16), v_ref[0, rows(j, half), h * d:(h + 1) * d],
                                        preferred_element_type=F32)
            cins[h] = cins[h] + cs[:, :1]
        return tuple(accs), tuple(cins)

    nfull = qi // 2
    carry = (tuple(jnp.zeros((tq, d), F32) for _ in range(nh)), tuple(jnp.zeros((tq, 1), F32) for _ in range(nh)))
    carry = block(nfull, carry, True)
    accs, _ = lax.fori_loop(0, nfull, lambda i, cr: block(nfull - 1 - i, cr, False), carry)
    for h in range(nh):
        o_ref[0, :, h * d:(h + 1) * d] = accs[h].astype(o_ref.dtype)


def sb_attention(p, n_heads, *, tq=256, nh=2):
    bsz, t_len, _ = p.shape
    d = SB_HEAD_DIM
    assert t_len % (2 * tq) == 0 and n_heads % nh == 0
    ng = n_heads // nh
    w = nh * d
    return pl.pallas_call(
        functools.partial(_sb_kernel, tq=tq, scale=d ** -0.5, nh=nh),
        out_shape=jax.ShapeDtypeStruct((bsz, t_len, n_heads * d), BF16),
        grid=(bsz, ng, t_len // tq),
        in_specs=[pl.BlockSpec((1, tq, w), lambda b, h, i: (b, i, h)),
                  pl.BlockSpec((1, t_len, w), lambda b, h, i: (b, 0, ng + h)),
                  pl.BlockSpec((1, t_len, w), lambda b, h, i: (b, 0, 2 * ng + h))],
        out_specs=pl.BlockSpec((1, tq, w), lambda b, h, i: (b, i, h)),
        compiler_params=_params("parallel", "parallel", "arbitrary"),
        name="sb_attention",
    )(p, p, p)


def _tri_inv_kernel(l_ref, t_ref, *, c):
    nb = c // SUBLANES
    sub = lax.broadcasted_iota(jnp.int32, (SUBLANES, LANES), 0)
    zero = jnp.zeros((SUBLANES, LANES), F32)
    for t in range(c):
        tb = t // SUBLANES
        accs = [zero] * (tb + 1)
        accs[tb] = jnp.where(sub == (t % SUBLANES), 1.0, 0.0)
        for j in range(t):
            lt = jnp.broadcast_to(l_ref[t, j:j + 1, :], (SUBLANES, LANES))
            for cb in range(j // SUBLANES + 1):
                accs[cb] = accs[cb] - lt * t_ref[j, cb * SUBLANES:(cb + 1) * SUBLANES, :]
        for cb in range(nb):
            t_ref[t, cb * SUBLANES:(cb + 1) * SUBLANES, :] = accs[cb] if cb <= tb else zero


def tri_inverse(l):
    shape = l.shape
    c = shape[-1]
    n_sys = math.prod(shape[:-2])
    n = -(-n_sys // LANES) * LANES
    lt = jnp.pad(l.reshape(n_sys, c * c), ((0, n - n_sys), (0, 0))).T.reshape(c, c, n)
    blk = pl.BlockSpec((c, c, LANES), lambda i: (0, 0, i))
    tt = pl.pallas_call(
        functools.partial(_tri_inv_kernel, c=c),
        out_shape=jax.ShapeDtypeStruct((c, c, n), F32),
        grid=(n // LANES,),
        in_specs=[blk],
        out_specs=blk,
        compiler_params=_params("parallel"),
        name="tri_inverse",
    )(lt)
    return tt.reshape(c * c, n).T[:n_sys].reshape(shape)


def _bdot(a, b):
    return jnp.dot(a, b, preferred_element_type=F32)


def _rwkv_decayed(lw, kk, a):
    c = lw.shape[0]
    g = _dot_exact_lhs(_tri(c, "ge").astype(BF16), lw)
    b = a * kk
    return g, b, kk * jnp.exp(g - lw), b * jnp.exp(-g)


def _rwkv_l_kernel(lw_ref, kk_ref, a_ref, l_ref):
    c = lw_ref.shape[1]
    nh = l_ref.shape[2]
    n = RW_HEAD_DIM
    _, _, kq, bk = _rwkv_decayed(lw_ref[0], kk_ref[0], a_ref[0])
    strict = _tri(c, "gt")
    lane = lax.broadcasted_iota(jnp.int32, (c, LANES), 1)
    sel = [lane < n, lane >= n]
    kqb, bkb = kq.astype(BF16), bk.astype(BF16)
    lhs = [jnp.where(sel[h % 2], kqb[:, (h // 2) * LANES:(h // 2 + 1) * LANES], 0) for h in range(nh)]
    prods = [_dot_nt(lhs[h], bkb[:, (h // 2) * LANES:(h // 2 + 1) * LANES]) for h in range(nh)]
    for h in range(nh):
        l_ref[0, 0, h] = jnp.where(strict, prods[h], 0.0)


def _rwkv_chunk_kernel(r_ref, lw_ref, k_ref, v_ref, kk_ref, a_ref, t_ref, y_ref, s_ref):
    c = lw_ref.shape[1]
    nh = t_ref.shape[2]
    n = RW_HEAD_DIM
    npair = nh // 2

    @pl.when(pl.program_id(1) == 0)
    def _():
        s_ref[...] = jnp.zeros_like(s_ref)

    r, lw, k, v, kk, a = (ref[0] for ref in (r_ref, lw_ref, k_ref, v_ref, kk_ref, a_ref))
    g, b, kq, bk = _rwkv_decayed(lw, kk, a)
    eg = jnp.exp(g)
    rq = r * eg
    kh = k * jnp.exp(-g)
    glast = g[c - 1:c, :]
    tail = jnp.exp(glast - g)
    kd = (k * tail).astype(BF16)
    bd = (b * tail).astype(BF16)
    pc = jnp.exp(glast)
    kqb, rqb, khb, bkb, vb = (x.astype(BF16) for x in (kq, rq, kh, bk, v))

    lane = lax.broadcasted_iota(jnp.int32, (c, LANES), 1)
    row = lax.broadcasted_iota(jnp.int32, (c, LANES), 0)
    lo_half = lane < n
    sel = [lo_half, jnp.logical_not(lo_half)]
    col = jnp.where(lo_half, lane, lane - n)
    m_uk = jnp.logical_and(lo_half, row > col)
    m_y = row >= col
    m_yb = jnp.logical_and(m_y, jnp.logical_not(lo_half))
    zeros = jnp.zeros((c, LANES), BF16)
    pl_ = lambda x, p: x[:, p * LANES:(p + 1) * LANES]

    heads = range(nh)
    lhs = [jnp.concatenate([jnp.where(sel[h % 2], pl_(kqb, h // 2), 0), jnp.where(sel[h % 2], pl_(rqb, h // 2), 0)],
                           axis=0) for h in heads]
    rhs = [jnp.concatenate([pl_(khb, p), pl_(bkb, p)], axis=0) for p in range(npair)]
    gm = [_dot_nt(lhs[h], rhs[h // 2]) for h in heads]
    a_u = [jnp.where(m_uk, gm[h][:c], 0.0).astype(BF16) for h in heads]
    a_y = [jnp.where(m_y, gm[h][c:], 0.0) for h in heads]
    a_ys = [jnp.where(lo_half, a_y[h], -a_y[h]).astype(BF16) for h in heads]
    a_yb = [jnp.where(m_yb, gm[h][c:], 0.0).astype(BF16) for h in heads]
    vz = [jnp.concatenate([pl_(vb, p), zeros], axis=0) for p in range(npair)]
    x1 = [_bdot(a_u[h], vz[h // 2]).astype(BF16) for h in heads]
    tb = [t_ref[0, 0, h].astype(BF16) for h in heads]
    tw = [_bdot(tb[h], jnp.concatenate([pl_(kqb, h // 2), x1[h]], axis=1)).astype(BF16) for h in heads]
    qe = [pl_(rq, h // 2) - _bdot(a_yb[h], jnp.concatenate([zeros, tw[h][:, :LANES]], axis=0)) for h in heads]
    yl = [_bdot(a_ys[h], jnp.concatenate([pl_(vb, h // 2), tw[h][:, LANES:]], axis=0)) for h in heads]
    r2 = lax.broadcasted_iota(jnp.int32, (LANES, LANES), 0)
    c2 = lax.broadcasted_iota(jnp.int32, (LANES, LANES), 1)
    bdiag = (r2 < n) == (c2 < n)
    pairs = range(npair)
    merge = lambda xs, p: jnp.where(lo_half, xs[2 * p], xs[2 * p + 1])
    wq = [merge([t[:, :LANES] for t in tw], p) for p in pairs]
    ut = [merge([t[:, LANES:] for t in tw], p) for p in pairs]
    qeff = [merge(qe, p).astype(BF16) for p in pairs]
    mc = [jnp.where(bdiag, _dot_tn(wq[p], pl_(bd, p)), 0.0).astype(BF16) for p in pairs]
    nn = [jnp.where(bdiag, _dot_tn(pl_(vb, p), pl_(kd, p)) - _dot_tn(ut[p], pl_(bd, p)), 0.0) for p in pairs]
    s = [s_ref[p] for p in pairs]
    sb = [x.astype(BF16) for x in s]
    y = [_dot_nt(qeff[p], sb[p]) for p in pairs]
    sm = [_bdot(sb[p], mc[p]) for p in pairs]
    for p in pairs:
        y_ref[0, :, p * LANES:(p + 1) * LANES] = y[p] + merge(yl, p)
        s_ref[p] = s[p] * pl_(pc, p) - sm[p] + nn[p]


def rwkv7_recurrence(r, lw, k, v, kk, a):
    bsz, t_len, width = r.shape
    nh = width // RW_HEAD_DIM
    npair = nh // 2
    c = min(CHUNK, t_len)
    nc = t_len // c
    seq = pl.BlockSpec((1, c, width), lambda b, i: (b, i, 0))
    mat = pl.BlockSpec((1, 1, nh, c, c), lambda b, i: (b, i, 0, 0, 0))
    l = pl.pallas_call(
        _rwkv_l_kernel,
        out_shape=jax.ShapeDtypeStruct((bsz, nc, nh, c, c), F32),
        grid=(bsz, nc),
        in_specs=[seq, seq, seq],
        out_specs=mat,
        compiler_params=_params("parallel", "parallel"),
        name="rwkv_l",
    )(lw, kk, a)
    tinv = tri_inverse(l)
    return pl.pallas_call(
        _rwkv_chunk_kernel,
        out_shape=jax.ShapeDtypeStruct((bsz, t_len, width), F32),
        grid=(bsz, nc),
        in_specs=[seq] * 6 + [mat],
        out_specs=seq,
        scratch_shapes=[pltpu.VMEM((npair, LANES, LANES), F32)],
        compiler_params=_params("parallel", "arbitrary"),
        name="rwkv_chunk",
    )(r, lw, k, v, kk, a, tinv)


def _gdn_l_kernel(k_ref, beta_ref, gcol_ref, grow_ref, l_ref, *, hv, rep):
    c = k_ref.shape[1]
    dh = GDN_HEAD_DIM
    incl = _tri(c, "ge")
    strict = _tri(c, "gt")
    gc_col = _dot_exact_lhs(incl.astype(BF16), gcol_ref[0, 0, 0])
    gc_row = _dot_exact_rhs(grow_ref[0, 0, 0], _tri(c, "le").astype(BF16))
    beta_all = beta_ref[0, 0, 0]
    ks = [k_ref[0, :, i * dh:(i + 1) * dh] for i in range(hv // rep)]
    kkt = [_dot_nt(x, x) for x in ks]
    decay = [jnp.exp(jnp.minimum(gc_col[:, j:j + 1] - gc_row[j:j + 1, :], 0.0)) for j in range(hv)]
    for j in range(hv):
        l_ref[0, 0, j] = jnp.where(strict, kkt[j // rep] * beta_all[:, j:j + 1] * decay[j], 0.0)


def _gdn_chunk_kernel(q_ref, k_ref, v_ref, beta_ref, gcol_ref, grow_ref, t_ref, z_ref, nw_ref, o_ref, s_ref,
                      *, hv, rep):
    c = k_ref.shape[1]
    dh = GDN_HEAD_DIM

    @pl.when(pl.program_id(2) == 0)
    def _():
        s_ref[...] = jnp.zeros_like(s_ref)

    incl = _tri(c, "ge")
    gc_col = _dot_exact_lhs(incl.astype(BF16), gcol_ref[0, 0, 0])
    gc_row = _dot_exact_rhs(grow_ref[0, 0, 0], _tri(c, "le").astype(BF16))
    beta_all = beta_ref[0, 0, 0]
    hk = hv // rep
    ks = [k_ref[0, :, i * dh:(i + 1) * dh] for i in range(hk)]
    qs = [q_ref[0, :, i * dh:(i + 1) * dh] for i in range(hk)]
    qk = [_dot_nt(qs[i], ks[i]) for i in range(hk)]
    heads = range(hv)
    gcc = [gc_col[:, j:j + 1] for j in heads]
    decay = [jnp.where(incl, jnp.exp(jnp.minimum(gcc[j] - gc_row[j:j + 1, :], 0.0)), 0.0) for j in heads]
    egc = [jnp.exp(gcc[j]) for j in heads]
    glast = [gcc[j][c - 1:c, :] for j in heads]
    beta = [beta_all[:, j:j + 1] for j in heads]
    attn = [(qk[j // rep] * decay[j]).astype(BF16) for j in heads]
    kd = [(ks[j // rep] * jnp.exp(glast[j] - gcc[j])).astype(BF16) for j in heads]
    rhs = [jnp.concatenate([v_ref[0, :, j * dh:(j + 1) * dh] * beta[j],
                            ks[j // rep] * (beta[j] * egc[j])], axis=1).astype(BF16) for j in heads]
    uw = [jnp.dot(t_ref[0, 0, j].astype(BF16), rhs[j], preferred_element_type=F32).astype(BF16) for j in heads]
    us = [x[:, :dh] for x in uw]
    ws = [x[:, dh:] for x in uw]
    qeff = [(qs[j // rep] * egc[j] - jnp.dot(attn[j], ws[j], preferred_element_type=F32)).astype(BF16)
            for j in heads]
    ol = [jnp.dot(attn[j], us[j], preferred_element_type=F32) for j in heads]
    mc = [_dot_tn(kd[j], ws[j]).astype(BF16) for j in heads]
    nn = [_dot_tn(kd[j], us[j]) for j in heads]
    s = [s_ref[j] for j in heads]
    sb = [x.astype(BF16) for x in s]
    o = [jnp.dot(qeff[j], sb[j], preferred_element_type=F32) for j in heads]
    ms = [jnp.dot(mc[j], sb[j], preferred_element_type=F32) for j in heads]
    for j in heads:
        sl = slice(j * dh, (j + 1) * dh)
        oj = o[j] + ol[j]
        oj = oj * lax.rsqrt(jnp.mean(oj * oj, axis=-1, keepdims=True) + GDN_NORM_EPS) * nw_ref[...]
        z = z_ref[0, :, sl]
        o_ref[0, :, sl] = (oj * (z * jax.nn.sigmoid(z))).astype(o_ref.dtype)
        s_ref[j] = s[j] * jnp.exp(glast[j]) - ms[j] + nn[j]


def gdn_recurrence(q, k, v, g, beta, p_main, z_col0, norm_w, *, hv_group=16):
    bsz, t_len, kw = q.shape
    dh = GDN_HEAD_DIM
    hk = kw // dh
    hv = v.shape[-1] // dh
    rep = hv // hk
    hvg = min(hv_group, hv)
    ng = hv // hvg
    c = min(CHUNK, t_len)
    nc = t_len // c
    col = lambda x: x.reshape(bsz, nc, c, ng, hvg).transpose(0, 3, 1, 2, 4)
    g_col, beta_col = col(g), col(beta)
    g_row = g_col.transpose(0, 1, 2, 4, 3)
    qk_spec = pl.BlockSpec((1, c, hvg // rep * dh), lambda b, h, i: (b, i, h))
    v_spec = pl.BlockSpec((1, c, hvg * dh), lambda b, h, i: (b, i, h))
    col_spec = pl.BlockSpec((1, 1, 1, c, hvg), lambda b, h, i: (b, h, i, 0, 0))
    row_spec = pl.BlockSpec((1, 1, 1, hvg, c), lambda b, h, i: (b, h, i, 0, 0))
    mat = pl.BlockSpec((1, 1, hvg, c, c), lambda b, h, i: (b, i, h, 0, 0))
    l = pl.pallas_call(
        functools.partial(_gdn_l_kernel, hv=hvg, rep=rep),
        out_shape=jax.ShapeDtypeStruct((bsz, nc, hv, c, c), F32),
        grid=(bsz, ng, nc),
        in_specs=[qk_spec, col_spec, col_spec, row_spec],
        out_specs=mat,
        compiler_params=_params("parallel", "parallel", "parallel"),
        name="gdn_l",
    )(k, beta_col, g_col, g_row)
    tinv = tri_inverse(l)
    z0 = z_col0 // (hvg * dh)
    z_spec = pl.BlockSpec((1, c, hvg * dh), lambda b, h, i: (b, i, z0 + h))
    return pl.pallas_call(
        functools.partial(_gdn_chunk_kernel, hv=hvg, rep=rep),
        out_shape=jax.ShapeDtypeStruct((bsz, t_len, hv * dh), BF16),
        grid=(bsz, ng, nc),
        in_specs=[qk_spec, qk_spec, v_spec, col_spec, col_spec, row_spec, mat, z_spec,
                  pl.BlockSpec((1, dh), lambda b, h, i: (0, 0))],
        out_specs=v_spec,
        scratch_shapes=[pltpu.VMEM((hvg, dh, dh), F32)],
        compiler_params=_params("parallel", "parallel", "arbitrary"),
        name="gdn_chunk",
    )(q, k, v, beta_col, g_col, g_row, tinv, p_main, norm_w.reshape(1, dh))


def _heads(t, hd):
    return t.reshape(t.shape[:-1] + (t.shape[-1] // hd, hd))


def _l2n(x, eps):
    return x * lax.rsqrt(jnp.sum(x * x, axis=-1, keepdims=True) + eps)


def _softplus(x):
    return jnp.maximum(x, 0.0) + jnp.log(1.0 + jnp.exp(-jnp.abs(x)))


def _head_sum_matrix(n, value):
    r = lax.broadcasted_iota(jnp.int32, (LANES, LANES), 0) // n
    c = lax.broadcasted_iota(jnp.int32, (LANES, LANES), 1) // n
    return jnp.where(r == c, value, 0.0).astype(BF16)


def _rwkv_prep_kernel(*refs, has_vres):
    if has_vres:
        (x_ref, halo_ref, mu_ref, w0_ref, wup_ref, a0_ref, aup_ref, gup_ref, kk_ref, ka_ref,
         vf_ref, v0_ref, vup_ref, r_out, lw_out, k_out, v_out, kkn_out, a_out, g_out) = refs
    else:
        (x_ref, halo_ref, mu_ref, w0_ref, wup_ref, a0_ref, aup_ref, gup_ref, kk_ref, ka_ref,
         r_out, lw_out, k_out, v_out, kkn_out, a_out, g_out) = refs
    x = x_ref[0]
    tr = x.shape[0]
    width = w0_ref.shape[1]
    halo = jnp.where(pl.program_id(1) > 0, halo_ref[0], 0.0)
    prev = jnp.concatenate([halo, x], axis=0)[SUBLANES - 1:SUBLANES - 1 + tr]
    xs = x + (prev - x) * mu_ref[...]
    o = 3 * width
    n_w, n_a, n_g = wup_ref.shape[0], aup_ref.shape[0], gup_ref.shape[0]
    r, k, v = xs[:, :width], xs[:, width:2 * width], xs[:, 2 * width:o]
    w_lo, a_lo, g_lo = xs[:, o:o + n_w], xs[:, o + n_w:o + n_w + n_a], xs[:, o + n_w + n_a:o + n_w + n_a + n_g]
    if has_vres:
        v_lo = xs[:, o + n_w + n_a + n_g:o + n_w + n_a + n_g + vup_ref.shape[0]]
        v = v + (vf_ref[0] - v) * jax.nn.sigmoid(v0_ref[...] + _dot(v_lo, vup_ref[...]))
    w_log = -_softplus(-(w0_ref[...] + _dot(jnp.tanh(w_lo), wup_ref[...]))) - 0.5
    a = jax.nn.sigmoid(a0_ref[...] + _dot(a_lo, aup_ref[...]))
    kkr = k * kk_ref[...]
    sq = kkr * kkr
    ones = _head_sum_matrix(RW_HEAD_DIM, 1.0)
    for p in range(width // LANES):
        sl = slice(p * LANES, (p + 1) * LANES)
        ss = _dot_exact_rhs(sq[:, sl], ones)
        kkn_out[0, :, sl] = kkr[:, sl] * lax.rsqrt(ss + RW_KK_EPS)
    r_out[0] = r
    lw_out[0] = -jnp.exp(w_log)
    k_out[0] = k * (1.0 + (a - 1.0) * ka_ref[...])
    v_out[0] = v
    a_out[0] = a
    g_out[0] = _dot(jax.nn.sigmoid(g_lo), gup_ref[...])


def rwkv_prep(p, mu, w0, w_up, a0, a_up, g_up, k_k, k_a, v_first, v0, v_up, *, tr=256):
    bsz, t_len, wp = p.shape
    width = w0.shape[-1]
    tr = min(tr, t_len)
    has_vres = v_first is not None
    rows = pl.BlockSpec((1, tr, wp), lambda b, i: (b, i, 0))
    halo = pl.BlockSpec((1, SUBLANES, wp), lambda b, i: (b, jnp.maximum(i * (tr // SUBLANES) - 1, 0), 0))
    full = lambda a: pl.BlockSpec(a.shape, lambda b, i: (0,) * a.ndim)
    seq = pl.BlockSpec((1, tr, width), lambda b, i: (b, i, 0))
    vec = lambda a: a.reshape(1, -1)
    args = [p, p, vec(mu), vec(w0), w_up, vec(a0), a_up, g_up, vec(k_k), vec(k_a)]
    specs = [rows, halo] + [full(a) for a in args[2:]]
    if has_vres:
        extra = [v_first, vec(v0), v_up]
        args += extra
        specs += [seq, full(extra[1]), full(extra[2])]
    return pl.pallas_call(
        functools.partial(_rwkv_prep_kernel, has_vres=has_vres),
        out_shape=tuple(jax.ShapeDtypeStruct((bsz, t_len, width), F32) for _ in range(7)),
        grid=(bsz, t_len // tr),
        in_specs=specs,
        out_specs=tuple(seq for _ in range(7)),
        compiler_params=_params("parallel", "parallel"),
        name="rwkv_prep",
    )(*args)


def _rwkv_post_kernel(y_ref, r_ref, k_ref, v_ref, g_ref, rk_ref, lg_ref, lb_ref, o_ref):
    n = RW_HEAD_DIM
    mean_m = _head_sum_matrix(n, 1.0 / n)
    sum_m = _head_sum_matrix(n, 1.0)
    for p in range(y_ref.shape[2] // LANES):
        sl = slice(p * LANES, (p + 1) * LANES)
        y = y_ref[0, :, sl]
        d = y - _dot_exact_rhs(y, mean_m)
        var = _dot_exact_rhs(d * d, mean_m)
        yn = d * lax.rsqrt(var + RW_LNX_EPS) * lg_ref[:, sl] + lb_ref[:, sl]
        bonus = _dot_exact_rhs(r_ref[0, :, sl] * k_ref[0, :, sl] * rk_ref[:, sl], sum_m) * v_ref[0, :, sl]
        o_ref[0, :, sl] = ((yn + bonus) * g_ref[0, :, sl]).astype(o_ref.dtype)


def rwkv_post(y, r, k, v, g, r_k, lnx_g, lnx_b, *, tr=256):
    bsz, t_len, width = y.shape
    tr = min(tr, t_len)
    seq = pl.BlockSpec((1, tr, width), lambda b, i: (b, i, 0))
    vec = pl.BlockSpec((1, width), lambda b, i: (0, 0))
    return pl.pallas_call(
        _rwkv_post_kernel,
        out_shape=jax.ShapeDtypeStruct((bsz, t_len, width), BF16),
        grid=(bsz, t_len // tr),
        in_specs=[seq] * 5 + [vec] * 3,
        out_specs=seq,
        compiler_params=_params("parallel", "parallel"),
        name="rwkv_post",
    )(y, r, k, v, g, r_k.reshape(1, width), lnx_g.reshape(1, width), lnx_b.reshape(1, width))


def _pad_rows(w, mult):
    return jnp.pad(w, ((0, (-w.shape[0]) % mult), (0, 0)))


def rwkv7_group(xb, w_rw, mu, w0, w_up, a0, a_up, g_up, k_k, k_a, r_k, lnx_g, lnx_b, v_first, v0, v_up,
                bsz, t_len):
    width = w0.shape[-1]
    lora = [w_up.shape[0], a_up.shape[0], g_up.shape[0]] + ([v_up.shape[0]] if v_up is not None else [])
    cols, mus, off = [w_rw[:, :3 * width]], [mu[:3 * width]], 3 * width
    for n in lora:
        cols.append(_pad_cols(w_rw[:, off:off + n], LANES))
        mus.append(jnp.pad(mu[off:off + n], (0, (-n) % LANES)))
        off += n
    w_lay = _pad_cols(jnp.concatenate(cols, axis=1), 256)
    mu_lay = jnp.pad(jnp.concatenate(mus), (0, w_lay.shape[1] - sum(m.shape[0] for m in mus)))
    p = matmul(xb, w_lay.astype(BF16), name="proj_rw").reshape(bsz, t_len, -1)
    up = lambda w: _pad_rows(w, LANES).astype(BF16)
    r, lw, k, v, kk, a, g = rwkv_prep(p, mu_lay, w0, up(w_up), a0, up(a_up), up(g_up), k_k, k_a,
                                      v_first, v0, None if v_up is None else up(v_up))
    if v_first is None:
        v_first = v
    y = rwkv7_recurrence(r, lw, k, v, kk, a)
    return rwkv_post(y, r, k, v, g, r_k.reshape(-1), lnx_g, lnx_b), v_first


def _gdn_prep_kernel(x_ref, halo_ref, w_ref, o_ref, *, mode):
    x = x_ref[0]
    tr = x.shape[0]
    taps = w_ref.shape[0]
    halo = jnp.where(pl.program_id(1) > 0, halo_ref[0], 0.0)
    xc = jnp.concatenate([halo, x], axis=0)
    w = w_ref[...]
    y = sum(w[i:i + 1, :] * xc[SUBLANES - taps + 1 + i:SUBLANES - taps + 1 + i + tr] for i in range(taps))
    y = y * jax.nn.sigmoid(y)
    dh = GDN_HEAD_DIM
    for h in range(y.shape[1] // dh):
        seg = y[:, h * dh:(h + 1) * dh]
        if mode != "v":
            seg = seg * lax.rsqrt(jnp.sum(seg * seg, axis=-1, keepdims=True) + GDN_QK_EPS)
        if mode == "q":
            seg = seg * (dh ** -0.5)
        o_ref[0, :, h * dh:(h + 1) * dh] = seg.astype(o_ref.dtype)


def gdn_prep(p, conv_w, col0, ncols, mode, *, tr=512, tw=1024):
    bsz, t_len, _ = p.shape
    tr, tw = min(tr, t_len), min(tw, ncols)
    c0 = col0 // tw
    rows = pl.BlockSpec((1, tr, tw), lambda b, i, j: (b, i, c0 + j))
    halo = pl.BlockSpec((1, SUBLANES, tw), lambda b, i, j: (b, jnp.maximum(i * (tr // SUBLANES) - 1, 0), c0 + j))
    return pl.pallas_call(
        functools.partial(_gdn_prep_kernel, mode=mode),
        out_shape=jax.ShapeDtypeStruct((bsz, t_len, ncols), BF16),
        grid=(bsz, t_len // tr, ncols // tw),
        in_specs=[rows, halo, pl.BlockSpec((conv_w.shape[0], tw), lambda b, i, j: (0, c0 + j))],
        out_specs=pl.BlockSpec((1, tr, tw), lambda b, i, j: (b, i, j)),
        compiler_params=_params("parallel", "parallel", "parallel"),
        name="gdn_prep_" + mode,
    )(p, p, conv_w)


def gated_deltanet(p_main, ba, conv_w, a_log, dt_bias, norm_w, key_width):
    hv = a_log.shape[0]
    conv_ch = conv_w.shape[1]
    q = gdn_prep(p_main, conv_w, 0, key_width, "q")
    k = gdn_prep(p_main, conv_w, key_width, key_width, "k")
    v = gdn_prep(p_main, conv_w, 2 * key_width, conv_ch - 2 * key_width, "v")
    beta = jax.nn.sigmoid(ba[..., :hv])
    g = -jnp.exp(a_log) * jax.nn.softplus(ba[..., hv:2 * hv] + dt_bias)
    return gdn_recurrence(q, k, v, g, beta, p_main, conv_ch, norm_w)


def _pad_cols(w, mult):
    n = w.shape[1]
    return jnp.pad(w, ((0, 0), (0, (-n) % mult)))


def kernel(x, ev_w_in, ev_shift, ev_w0, ev_w_up, ev_a0, ev_a_up, ev_g_up, ev_k_k, ev_k_a, ev_r_k, ev_lnx_g, ev_lnx_b, vres_w_down, vres_shift, vres_v0, vres_v_up, ev_w_out, od_w_in, od_conv, od_a_log, od_dt_bias, od_norm_w, od_w_out, ln1_g, ln1_b, mlp_w1, mlp_w2, ln2_g, ln2_b):
    bsz, t_len, d_model = x.shape
    depth = ln1_g.shape[0]
    alpha = (2 * depth) ** 0.25
    m = bsz * t_len
    rw_width = ev_w0.shape[1]
    sb_width = ev_w_out.shape[1] - rw_width
    sb_heads = sb_width // SB_HEAD_DIM
    sb_cols = 3 * sb_width
    gdn_hv = od_a_log.shape[1]
    val_width = gdn_hv * GDN_HEAD_DIM
    conv_ch = od_conv.shape[2]
    key_width = (conv_ch - val_width) // 2

    x = x.reshape(m, d_model)
    xb = x.astype(BF16)
    v_first = None
    for layer in range(depth):
        if layer % 2 == 0:
            e = layer // 2
            w_sb = ev_w_in[e][:, :sb_cols]
            w_rw = ev_w_in[e][:, sb_cols:]
            mu = ev_shift[e]
            v0 = v_up = None
            if e > 0:
                w_rw = jnp.concatenate([w_rw, vres_w_down[e - 1]], axis=1)
                mu = jnp.concatenate([mu, vres_shift[e - 1]])
                v0, v_up = vres_v0[e - 1], vres_v_up[e - 1]
            p_sb = matmul(xb, w_sb.astype(BF16), out_dtype=BF16, name="proj_sb")
            o_sb = sb_attention(p_sb.reshape(bsz, t_len, sb_cols), sb_heads)
            o_rw, v_first = rwkv7_group(xb, w_rw, mu, ev_w0[e], ev_w_up[e], ev_a0[e], ev_a_up[e], ev_g_up[e],
                                        ev_k_k[e], ev_k_a[e], ev_r_k[e], ev_lnx_g[e], ev_lnx_b[e],
                                        v_first, v0, v_up, bsz, t_len)
            x, xb = matmul_ln([o_sb.reshape(m, -1), o_rw.reshape(m, -1)], ev_w_out[e].astype(BF16), x,
                              ln1_g[layer], ln1_b[layer], alpha, tm=256, name="proj_out_ln")
        else:
            o = layer // 2
            w_main = od_w_in[o][:, :conv_ch + val_width]
            w_ba = od_w_in[o][:, conv_ch + val_width:]
            p_main = matmul(xb, w_main.astype(BF16), name="proj_gdn").reshape(bsz, t_len, -1)
            p_ba = matmul(xb, _pad_cols(w_ba, LANES).astype(BF16), name="proj_gdn_gates")[:, :2 * gdn_hv]
            mixed = gated_deltanet(p_main, p_ba.reshape(bsz, t_len, -1), od_conv[o], od_a_log[o],
                                   od_dt_bias[o], od_norm_w[o], key_width)
            x, xb = matmul_ln([mixed.reshape(m, -1)], od_w_out[o].astype(BF16), x, ln1_g[layer], ln1_b[layer],
                              alpha, tm=512, tk=1024, name="proj_out_ln")
        hid = matmul(xb, mlp_w1[layer].astype(BF16), act="relu2", out_dtype=BF16, name="mlp_up")
        x, xb = matmul_ln([hid], mlp_w2[layer].astype(BF16), x, ln2_g[layer], ln2_b[layer], alpha,
                          tm=512, tk=1024, name="mlp_down_ln")
    return x.reshape(bsz, t_len, d_model)
```

```python
import functools
import math

import jax
import jax.numpy as jnp
from jax import lax
from jax.experimental import pallas as pl
from jax.experimental.pallas import tpu as pltpu

F32 = jnp.float32
BF16 = jnp.bfloat16

LANES = 128
SUBLANES = 8
CHUNK = 64
VMEM_LIMIT = 48 * 1024 * 1024

SB_HEAD_DIM = 128
RW_HEAD_DIM = 64
GDN_HEAD_DIM = 128
GDN_CONV = 4
RW_LNX_EPS = 64e-5
RW_KK_EPS = 1e-12
GDN_NORM_EPS = 1e-6
GDN_QK_EPS = 1e-6
LN_EPS = 1e-5


def _params(*sem):
    return pltpu.CompilerParams(dimension_semantics=sem, vmem_limit_bytes=VMEM_LIMIT)


def _dot(a, b):
    return jnp.dot(a.astype(BF16), b.astype(BF16), preferred_element_type=F32)


def _dot_nt(a, b):
    return lax.dot_general(a.astype(BF16), b.astype(BF16), (((1,), (1,)), ((), ())),
                           preferred_element_type=F32)


def _dot_tn(a, b):
    return lax.dot_general(a.astype(BF16), b.astype(BF16), (((0,), (0,)), ((), ())),
                           preferred_element_type=F32)


def _split3(x):
    x1 = x.astype(BF16)
    r1 = x - x1.astype(F32)
    x2 = r1.astype(BF16)
    x3 = (r1 - x2.astype(F32)).astype(BF16)
    return x1, x2, x3


def _dot_exact_lhs(m01, x):
    x1, x2, x3 = _split3(x)
    d = lambda t: jnp.dot(m01, t, preferred_element_type=F32)
    return d(x1) + d(x2) + d(x3)


def _dot_exact_rhs(x, m01):
    x1, x2, x3 = _split3(x)
    d = lambda t: jnp.dot(t, m01, preferred_element_type=F32)
    return d(x1) + d(x2) + d(x3)


def _tri(n, kind):
    r = lax.broadcasted_iota(jnp.int32, (n, n), 0)
    c = lax.broadcasted_iota(jnp.int32, (n, n), 1)
    return {"ge": r >= c, "gt": r > c, "le": r <= c}[kind]


def _mm_kernel(a_ref, w_ref, o_ref, *scratch, act, nk):
    prod = jnp.dot(a_ref[...], w_ref[...], preferred_element_type=F32)

    def finish(acc):
        if act == "relu2":
            acc = jnp.square(jnp.maximum(acc, 0.0))
        o_ref[...] = acc.astype(o_ref.dtype)

    if nk == 1:
        finish(prod)
    else:
        acc_ref, = scratch
        k = pl.program_id(2)

        @pl.when(k == 0)
        def _():
            acc_ref[...] = prod

        @pl.when(jnp.logical_and(k > 0, k < nk - 1))
        def _():
            acc_ref[...] += prod

        @pl.when(k == nk - 1)
        def _():
            finish(acc_ref[...] + prod)


def _pick(n, cands):
    for c in cands:
        if n % c == 0:
            return c
    raise ValueError(f"no tile for {n}")


def matmul(a, w, *, act=None, out_dtype=F32, name="mm"):
    m, k = a.shape
    k2, n = w.shape
    assert k == k2
    tm = _pick(m, (1024, 512, 256, 128, 64, 32, 16, 8))
    tn = _pick(n, (1024, 512, 256, 128))
    tk = _pick(k, (2048, 1024, 512, 256, 128))
    nk = k // tk
    scratch = [pltpu.VMEM((tm, tn), F32)] if nk > 1 else []
    return pl.pallas_call(
        functools.partial(_mm_kernel, act=act, nk=nk),
        out_shape=jax.ShapeDtypeStruct((m, n), out_dtype),
        grid=(m // tm, n // tn, nk),
        in_specs=[pl.BlockSpec((tm, tk), lambda i, j, kk: (i, kk)),
                  pl.BlockSpec((tk, tn), lambda i, j, kk: (kk, j))],
        out_specs=pl.BlockSpec((tm, tn), lambda i, j, kk: (i, j)),
        scratch_shapes=scratch,
        compiler_params=_params("parallel", "parallel", "arbitrary"),
        name=name,
    )(a, w)


def _residual_ln(x, h, g, b, alpha, o_ref, ob_ref):
    y = alpha * x + h
    mu = jnp.mean(y, axis=-1, keepdims=True)
    yc = y - mu
    var = jnp.mean(jnp.square(yc), axis=-1, keepdims=True)
    out = yc * lax.rsqrt(var + LN_EPS) * g + b
    o_ref[...] = out
    ob_ref[...] = out.astype(BF16)


def _mm_ln_kernel(*refs, alpha, n_lhs, nk):
    a_refs, w_refs = refs[:n_lhs], refs[n_lhs:2 * n_lhs]
    x_ref, g_ref, b_ref, o_ref, ob_ref = refs[2 * n_lhs:2 * n_lhs + 5]
    prod = sum(jnp.dot(a[...], w[...], preferred_element_type=F32) for a, w in zip(a_refs, w_refs))
    if nk == 1:
        _residual_ln(x_ref[...], prod, g_ref[...], b_ref[...], alpha, o_ref, ob_ref)
    else:
        acc_ref = refs[-1]
        k = pl.program_id(1)

        @pl.when(k == 0)
        def _():
            acc_ref[...] = prod

        @pl.when(jnp.logical_and(k > 0, k < nk - 1))
        def _():
            acc_ref[...] += prod

        @pl.when(k == nk - 1)
        def _():
            _residual_ln(x_ref[...], acc_ref[...] + prod, g_ref[...], b_ref[...], alpha, o_ref, ob_ref)


def matmul_ln(lhs, w, x, g, b, alpha, *, tm, tk=None, name="mm_ln"):
    m, d = x.shape
    n_lhs = len(lhs)
    kw = lhs[0].shape[1]
    assert all(a.shape == (m, kw) for a in lhs) and w.shape == (n_lhs * kw, d)
    tm = min(tm, m)
    if tk is None:
        nk, tk = 1, kw
        a_specs = [pl.BlockSpec((tm, kw), lambda i, kk: (i, 0)) for _ in lhs]
        w_specs = [pl.BlockSpec((kw, d), lambda i, kk, j=j: (j, 0)) for j in range(n_lhs)]
    else:
        assert n_lhs == 1 and kw % tk == 0
        nk = kw // tk
        a_specs = [pl.BlockSpec((tm, tk), lambda i, kk: (i, kk))]
        w_specs = [pl.BlockSpec((tk, d), lambda i, kk: (kk, 0))]
    row = pl.BlockSpec((tm, d), lambda i, kk: (i, 0))
    vec = pl.BlockSpec((1, d), lambda i, kk: (0, 0))
    return pl.pallas_call(
        functools.partial(_mm_ln_kernel, alpha=alpha, n_lhs=n_lhs, nk=nk),
        out_shape=(jax.ShapeDtypeStruct((m, d), F32), jax.ShapeDtypeStruct((m, d), BF16)),
        grid=(m // tm, nk),
        in_specs=a_specs + w_specs + [row, vec, vec],
        out_specs=(row, row),
        scratch_shapes=[pltpu.VMEM((tm, d), F32)] if nk > 1 else [],
        compiler_params=_params("parallel", "arbitrary"),
        name=name,
    )(*lhs, *([w] * n_lhs), x, g.reshape(1, d), b.reshape(1, d))


def _split_hi_lo(x):
    hi = lax.bitcast_convert_type(lax.bitcast_convert_type(x, jnp.uint32) & jnp.uint32(0xFFFF0000), F32)
    return hi.astype(BF16), (x - hi).astype(BF16)


def _sb_kernel(q_ref, k_ref, v_ref, o_ref, *, tq, scale, nh):
    qi = pl.program_id(2)
    d = SB_HEAD_DIM
    tk = 2 * tq
    r = lax.broadcasted_iota(jnp.int32, (tq, tq), 0)
    c = lax.broadcasted_iota(jnp.int32, (tq, tq), 1)
    suffix = (r >= c).astype(BF16)
    suffix2 = jnp.concatenate([suffix, suffix], axis=0)
    qs = [q_ref[0, :, h * d:(h + 1) * d] for h in range(nh)]
    streams = [(h, half) for h in range(nh) for half in (1, 0)]

    def rows(j, half):
        return pl.ds(pl.multiple_of(j * tk + half * tq, tq), tq)

    def cumsum(z, keep):
        sp = jnp.maximum(z, 0.0) + jnp.log(1.0 + jnp.exp(-jnp.abs(z)))
        if keep is not None:
            sp = jnp.where(keep, sp, 0.0)
        hi, lo = _split_hi_lo(sp)
        return jnp.dot(jnp.concatenate([hi, lo], axis=1), suffix2, preferred_element_type=F32)

    def block(j, carry, diagonal):
        accs, cins = list(carry[0]), list(carry[1])
        zs = [lax.dot_general(qs[h], k_ref[0, rows(j, half), h * d:(h + 1) * d], (((1,), (1,)), ((), ())),
                              preferred_element_type=F32) * scale for h, half in streams]
        keeps = [(j * tk + half * tq + c < qi * tq + r) if diagonal else None for _, half in streams]
        css = [cumsum(z, keep) for z, keep in zip(zs, keeps)]
        for (h, half), z, cs, keep in zip(streams, zs, css, keeps):
            a = jnp.exp(z - cs - cins[h])
            if keep is not None:
                a = jnp.where(keep, a, 0.0)
            accs[h] = accs[h] + jnp.dot(a.astype(BF16), v_ref[0, rows(j, half), h * d:(h + 1) * d],
                                        preferred_element_type=F32)
            cins[h] = cins[h] + cs[:, :1]
        return tuple(accs), tuple(cins)

    nfull = qi // 2
    carry = (tuple(jnp.zeros((tq, d), F32) for _ in range(nh)), tuple(jnp.zeros((tq, 1), F32) for _ in range(nh)))
    carry = block(nfull, carry, True)
    accs, _ = lax.fori_loop(0, nfull, lambda i, cr: block(nfull - 1 - i, cr, False), carry)
    for h in range(nh):
        o_ref[0, :, h * d:(h + 1) * d] = accs[h].astype(o_ref.dtype)


def sb_attention(p, n_heads, *, tq=256, nh=2):
    bsz, t_len, _ = p.shape
    d = SB_HEAD_DIM
    assert t_len % (2 * tq) == 0 and n_heads % nh == 0
    ng = n_heads // nh
    w = nh * d
    return pl.pallas_call(
        functools.partial(_sb_kernel, tq=tq, scale=d ** -0.5, nh=nh),
        out_shape=jax.ShapeDtypeStruct((bsz, t_len, n_heads * d), BF16),
        grid=(bsz, ng, t_len // tq),
        in_specs=[pl.BlockSpec((1, tq, w), lambda b, h, i: (b, i, h)),
                  pl.BlockSpec((1, t_len, w), lambda b, h, i: (b, 0, ng + h)),
                  pl.BlockSpec((1, t_len, w), lambda b, h, i: (b, 0, 2 * ng + h))],
        out_specs=pl.BlockSpec((1, tq, w), lambda b, h, i: (b, i, h)),
        compiler_params=_params("parallel", "parallel", "arbitrary"),
        name="sb_attention",
    )(p, p, p)


def _tri_inv_kernel(l_ref, t_ref, lt_ref, tt_ref, *, c):
    nblk = c * c // LANES
    for b in range(nblk):
        lt_ref[b * LANES:(b + 1) * LANES, :] = l_ref[:, b * LANES:(b + 1) * LANES].T
    nb = c // SUBLANES
    sub = lax.broadcasted_iota(jnp.int32, (SUBLANES, LANES), 0)
    zero = jnp.zeros((SUBLANES, LANES), F32)
    for t in range(c):
        tb = t // SUBLANES
        accs = [zero] * (tb + 1)
        accs[tb] = jnp.where(sub == (t % SUBLANES), 1.0, 0.0)
        for j in range(t):
            lt = jnp.broadcast_to(lt_ref[t * c + j:t * c + j + 1, :], (SUBLANES, LANES))
            for cb in range(j // SUBLANES + 1):
                accs[cb] = accs[cb] - lt * tt_ref[j * c + cb * SUBLANES:j * c + (cb + 1) * SUBLANES, :]
        for cb in range(nb):
            tt_ref[t * c + cb * SUBLANES:t * c + (cb + 1) * SUBLANES, :] = accs[cb] if cb <= tb else zero
    for b in range(nblk):
        t_ref[:, b * LANES:(b + 1) * LANES] = tt_ref[b * LANES:(b + 1) * LANES, :].T


def tri_inverse(l):
    shape = l.shape
    c = shape[-1]
    n_sys = math.prod(shape[:-2])
    n = -(-n_sys // LANES) * LANES
    flat = l.reshape(n_sys, c * c)
    if n != n_sys:
        flat = jnp.pad(flat, ((0, n - n_sys), (0, 0)))
    blk = pl.BlockSpec((LANES, c * c), lambda i: (i, 0))
    out = pl.pallas_call(
        functools.partial(_tri_inv_kernel, c=c),
        out_shape=jax.ShapeDtypeStruct((n, c * c), F32),
        grid=(n // LANES,),
        in_specs=[blk],
        out_specs=blk,
        scratch_shapes=[pltpu.VMEM((c * c, LANES), F32), pltpu.VMEM((c * c, LANES), F32)],
        compiler_params=_params("parallel"),
        name="tri_inverse",
    )(flat)
    return (out if n == n_sys else out[:n_sys]).reshape(shape)


def _bdot(a, b):
    return jnp.dot(a, b, preferred_element_type=F32)


def _rwkv_decayed(lw, kk, a):
    c = lw.shape[0]
    g = _dot_exact_lhs(_tri(c, "ge").astype(BF16), lw)
    b = a * kk
    return g, b, kk * jnp.exp(g - lw), b * jnp.exp(-g)


def _rwkv_l_kernel(lw_ref, kk_ref, a_ref, l_ref):
    c = lw_ref.shape[1]
    nh = l_ref.shape[2]
    n = RW_HEAD_DIM
    _, _, kq, bk = _rwkv_decayed(lw_ref[0], kk_ref[0], a_ref[0])
    strict = _tri(c, "gt")
    lane = lax.broadcasted_iota(jnp.int32, (c, LANES), 1)
    sel = [lane < n, lane >= n]
    kqb, bkb = kq.astype(BF16), bk.astype(BF16)
    lhs = [jnp.where(sel[h % 2], kqb[:, (h // 2) * LANES:(h // 2 + 1) * LANES], 0) for h in range(nh)]
    prods = [_dot_nt(lhs[h], bkb[:, (h // 2) * LANES:(h // 2 + 1) * LANES]) for h in range(nh)]
    for h in range(nh):
        l_ref[0, 0, h] = jnp.where(strict, prods[h], 0.0)


def _rwkv_chunk_kernel(r_ref, lw_ref, k_ref, v_ref, kk_ref, a_ref, t_ref, y_ref, s_ref):
    c = lw_ref.shape[1]
    nh = t_ref.shape[2]
    n = RW_HEAD_DIM
    npair = nh // 2

    @pl.when(pl.program_id(1) == 0)
    def _():
        s_ref[...] = jnp.zeros_like(s_ref)

    r, lw, k, v, kk, a = (ref[0] for ref in (r_ref, lw_ref, k_ref, v_ref, kk_ref, a_ref))
    g, b, kq, bk = _rwkv_decayed(lw, kk, a)
    eg = jnp.exp(g)
    rq = r * eg
    kh = k * jnp.exp(-g)
    glast = g[c - 1:c, :]
    tail = jnp.exp(glast - g)
    kd = (k * tail).astype(BF16)
    bd = (b * tail).astype(BF16)
    pc = jnp.exp(glast)
    kqb, rqb, khb, bkb, vb = (x.astype(BF16) for x in (kq, rq, kh, bk, v))

    lane = lax.broadcasted_iota(jnp.int32, (c, LANES), 1)
    row = lax.broadcasted_iota(jnp.int32, (c, LANES), 0)
    lo_half = lane < n
    sel = [lo_half, jnp.logical_not(lo_half)]
    col = jnp.where(lo_half, lane, lane - n)
    m_uk = jnp.logical_and(lo_half, row > col)
    m_y = row >= col
    m_yb = jnp.logical_and(m_y, jnp.logical_not(lo_half))
    zeros = jnp.zeros((c, LANES), BF16)
    pl_ = lambda x, p: x[:, p * LANES:(p + 1) * LANES]

    heads = range(nh)
    lhs = [jnp.concatenate([jnp.where(sel[h % 2], pl_(kqb, h // 2), 0), jnp.where(sel[h % 2], pl_(rqb, h // 2), 0)],
                           axis=0) for h in heads]
    rhs = [jnp.concatenate([pl_(khb, p), pl_(bkb, p)], axis=0) for p in range(npair)]
    gm = [_dot_nt(lhs[h], rhs[h // 2]) for h in heads]
    a_u = [jnp.where(m_uk, gm[h][:c], 0.0).astype(BF16) for h in heads]
    a_y = [jnp.where(m_y, gm[h][c:], 0.0) for h in heads]
    a_ys = [jnp.where(lo_half, a_y[h], -a_y[h]).astype(BF16) for h in heads]
    a_yb = [jnp.where(m_yb, gm[h][c:], 0.0).astype(BF16) for h in heads]
    vz = [jnp.concatenate([pl_(vb, p), zeros], axis=0) for p in range(npair)]
    x1 = [_bdot(a_u[h], vz[h // 2]).astype(BF16) for h in heads]
    tb = [t_ref[0, 0, h].astype(BF16) for h in heads]
    tw = [_bdot(tb[h], jnp.concatenate([pl_(kqb, h // 2), x1[h]], axis=1)).astype(BF16) for h in heads]
    qe = [pl_(rq, h // 2) - _bdot(a_yb[h], jnp.concatenate([zeros, tw[h][:, :LANES]], axis=0)) for h in heads]
    yl = [_bdot(a_ys[h], jnp.concatenate([pl_(vb, h // 2), tw[h][:, LANES:]], axis=0)) for h in heads]
    r2 = lax.broadcasted_iota(jnp.int32, (LANES, LANES), 0)
    c2 = lax.broadcasted_iota(jnp.int32, (LANES, LANES), 1)
    bdiag = (r2 < n) == (c2 < n)
    pairs = range(npair)
    merge = lambda xs, p: jnp.where(lo_half, xs[2 * p], xs[2 * p + 1])
    wq = [merge([t[:, :LANES] for t in tw], p) for p in pairs]
    ut = [merge([t[:, LANES:] for t in tw], p) for p in pairs]
    qeff = [merge(qe, p).astype(BF16) for p in pairs]
    mc = [jnp.where(bdiag, _dot_tn(wq[p], pl_(bd, p)), 0.0).astype(BF16) for p in pairs]
    nn = [jnp.where(bdiag, _dot_tn(pl_(vb, p), pl_(kd, p)) - _dot_tn(ut[p], pl_(bd, p)), 0.0) for p in pairs]
    s = [s_ref[p] for p in pairs]
    sb = [x.astype(BF16) for x in s]
    y = [_dot_nt(qeff[p], sb[p]) for p in pairs]
    sm = [_bdot(sb[p], mc[p]) for p in pairs]
    for p in pairs:
        y_ref[0, :, p * LANES:(p + 1) * LANES] = y[p] + merge(yl, p)
        s_ref[p] = s[p] * pl_(pc, p) - sm[p] + nn[p]


def rwkv7_recurrence(r, lw, k, v, kk, a):
    bsz, t_len, width = r.shape
    nh = width // RW_HEAD_DIM
    npair = nh // 2
    c = min(CHUNK, t_len)
    nc = t_len // c
    seq = pl.BlockSpec((1, c, width), lambda b, i: (b, i, 0))
    mat = pl.BlockSpec((1, 1, nh, c, c), lambda b, i: (b, i, 0, 0, 0))
    l = pl.pallas_call(
        _rwkv_l_kernel,
        out_shape=jax.ShapeDtypeStruct((bsz, nc, nh, c, c), F32),
        grid=(bsz, nc),
        in_specs=[seq, seq, seq],
        out_specs=mat,
        compiler_params=_params("parallel", "parallel"),
        name="rwkv_l",
    )(lw, kk, a)
    tinv = tri_inverse(l)
    return pl.pallas_call(
        _rwkv_chunk_kernel,
        out_shape=jax.ShapeDtypeStruct((bsz, t_len, width), F32),
        grid=(bsz, nc),
        in_specs=[seq] * 6 + [mat],
        out_specs=seq,
        scratch_shapes=[pltpu.VMEM((npair, LANES, LANES), F32)],
        compiler_params=_params("parallel", "arbitrary"),
        name="rwkv_chunk",
    )(r, lw, k, v, kk, a, tinv)


def _gdn_l_kernel(k_ref, beta_ref, gcol_ref, grow_ref, l_ref, *, hv, rep):
    c = k_ref.shape[1]
    dh = GDN_HEAD_DIM
    incl = _tri(c, "ge")
    strict = _tri(c, "gt")
    gc_col = _dot_exact_lhs(incl.astype(BF16), gcol_ref[0, 0, 0])
    gc_row = _dot_exact_rhs(grow_ref[0, 0, 0], _tri(c, "le").astype(BF16))
    beta_all = beta_ref[0, 0, 0]
    ks = [k_ref[0, :, i * dh:(i + 1) * dh] for i in range(hv // rep)]
    kkt = [_dot_nt(x, x) for x in ks]
    decay = [jnp.exp(jnp.minimum(gc_col[:, j:j + 1] - gc_row[j:j + 1, :], 0.0)) for j in range(hv)]
    for j in range(hv):
        l_ref[0, 0, j] = jnp.where(strict, kkt[j // rep] * beta_all[:, j:j + 1] * decay[j], 0.0)


def _gdn_chunk_kernel(q_ref, k_ref, v_ref, beta_ref, gcol_ref, grow_ref, t_ref, z_ref, nw_ref, o_ref, s_ref,
                      *, hv, rep):
    c = k_ref.shape[1]
    dh = GDN_HEAD_DIM

    @pl.when(pl.program_id(2) == 0)
    def _():
        s_ref[...] = jnp.zeros_like(s_ref)

    incl = _tri(c, "ge")
    gc_col = _dot_exact_lhs(incl.astype(BF16), gcol_ref[0, 0, 0])
    gc_row = _dot_exact_rhs(grow_ref[0, 0, 0], _tri(c, "le").astype(BF16))
    beta_all = beta_ref[0, 0, 0]
    hk = hv // rep
    ks = [k_ref[0, :, i * dh:(i + 1) * dh] for i in range(hk)]
    qs = [q_ref[0, :, i * dh:(i + 1) * dh] for i in range(hk)]
    qk = [_dot_nt(qs[i], ks[i]) for i in range(hk)]
    heads = range(hv)
    gcc = [gc_col[:, j:j + 1] for j in heads]
    decay = [jnp.where(incl, jnp.exp(jnp.minimum(gcc[j] - gc_row[j:j + 1, :], 0.0)), 0.0) for j in heads]
    egc = [jnp.exp(gcc[j]) for j in heads]
    glast = [gcc[j][c - 1:c, :] for j in heads]
    beta = [beta_all[:, j:j + 1] for j in heads]
    attn = [(qk[j // rep] * decay[j]).astype(BF16) for j in heads]
    kd = [(ks[j // rep] * jnp.exp(glast[j] - gcc[j])).astype(BF16) for j in heads]
    rhs = [jnp.concatenate([v_ref[0, :, j * dh:(j + 1) * dh] * beta[j],
                            ks[j // rep] * (beta[j] * egc[j])], axis=1).astype(BF16) for j in heads]
    uw = [jnp.dot(t_ref[0, 0, j].astype(BF16), rhs[j], preferred_element_type=F32).astype(BF16) for j in heads]
    us = [x[:, :dh] for x in uw]
    ws = [x[:, dh:] for x in uw]
    qeff = [(qs[j // rep] * egc[j] - jnp.dot(attn[j], ws[j], preferred_element_type=F32)).astype(BF16)
            for j in heads]
    ol = [jnp.dot(attn[j], us[j], preferred_element_type=F32) for j in heads]
    mc = [_dot_tn(kd[j], ws[j]).astype(BF16) for j in heads]
    nn = [_dot_tn(kd[j], us[j]) for j in heads]
    s = [s_ref[j] for j in heads]
    sb = [x.astype(BF16) for x in s]
    o = [jnp.dot(qeff[j], sb[j], preferred_element_type=F32) for j in heads]
    ms = [jnp.dot(mc[j], sb[j], preferred_element_type=F32) for j in heads]
    for j in heads:
        sl = slice(j * dh, (j + 1) * dh)
        oj = o[j] + ol[j]
        oj = oj * lax.rsqrt(jnp.mean(oj * oj, axis=-1, keepdims=True) + GDN_NORM_EPS) * nw_ref[...]
        z = z_ref[0, :, sl].astype(F32)
        o_ref[0, :, sl] = (oj * (z * jax.nn.sigmoid(z))).astype(o_ref.dtype)
        s_ref[j] = s[j] * jnp.exp(glast[j]) - ms[j] + nn[j]


def gdn_recurrence(q, k, v, g, beta, p_main, z_col0, norm_w, *, hv_group=16):
    bsz, t_len, kw = q.shape
    dh = GDN_HEAD_DIM
    hk = kw // dh
    hv = v.shape[-1] // dh
    rep = hv // hk
    hvg = min(hv_group, hv)
    ng = hv // hvg
    c = min(CHUNK, t_len)
    nc = t_len // c
    col = lambda x: x.reshape(bsz, nc, c, ng, hvg).transpose(0, 3, 1, 2, 4)
    g_col, beta_col = col(g), col(beta)
    g_row = g_col.transpose(0, 1, 2, 4, 3)
    qk_spec = pl.BlockSpec((1, c, hvg // rep * dh), lambda b, h, i: (b, i, h))
    v_spec = pl.BlockSpec((1, c, hvg * dh), lambda b, h, i: (b, i, h))
    col_spec = pl.BlockSpec((1, 1, 1, c, hvg), lambda b, h, i: (b, h, i, 0, 0))
    row_spec = pl.BlockSpec((1, 1, 1, hvg, c), lambda b, h, i: (b, h, i, 0, 0))
    mat = pl.BlockSpec((1, 1, hvg, c, c), lambda b, h, i: (b, i, h, 0, 0))
    l = pl.pallas_call(
        functools.partial(_gdn_l_kernel, hv=hvg, rep=rep),
        out_shape=jax.ShapeDtypeStruct((bsz, nc, hv, c, c), F32),
        grid=(bsz, ng, nc),
        in_specs=[qk_spec, col_spec, col_spec, row_spec],
        out_specs=mat,
        compiler_params=_params("parallel", "parallel", "parallel"),
        name="gdn_l",
    )(k, beta_col, g_col, g_row)
    tinv = tri_inverse(l)
    z0 = z_col0 // (hvg * dh)
    z_spec = pl.BlockSpec((1, c, hvg * dh), lambda b, h, i: (b, i, z0 + h))
    return pl.pallas_call(
        functools.partial(_gdn_chunk_kernel, hv=hvg, rep=rep),
        out_shape=jax.ShapeDtypeStruct((bsz, t_len, hv * dh), BF16),
        grid=(bsz, ng, nc),
        in_specs=[qk_spec, qk_spec, v_spec, col_spec, col_spec, row_spec, mat, z_spec,
                  pl.BlockSpec((1, dh), lambda b, h, i: (0, 0))],
        out_specs=v_spec,
        scratch_shapes=[pltpu.VMEM((hvg, dh, dh), F32)],
        compiler_params=_params("parallel", "parallel", "arbitrary"),
        name="gdn_chunk",
    )(q, k, v, beta_col, g_col, g_row, tinv, p_main, norm_w.reshape(1, dh))


def _heads(t, hd):
    return t.reshape(t.shape[:-1] + (t.shape[-1] // hd, hd))


def _l2n(x, eps):
    return x * lax.rsqrt(jnp.sum(x * x, axis=-1, keepdims=True) + eps)


def _softplus(x):
    return jnp.maximum(x, 0.0) + jnp.log(1.0 + jnp.exp(-jnp.abs(x)))


def _head_sum_matrix(n, value):
    r = lax.broadcasted_iota(jnp.int32, (LANES, LANES), 0) // n
    c = lax.broadcasted_iota(jnp.int32, (LANES, LANES), 1) // n
    return jnp.where(r == c, value, 0.0).astype(BF16)


def _rwkv_prep_kernel(*refs, has_vres):
    if has_vres:
        (x_ref, halo_ref, mu_ref, w0_ref, wup_ref, a0_ref, aup_ref, gup_ref, kk_ref, ka_ref,
         vf_ref, v0_ref, vup_ref, r_out, lw_out, k_out, v_out, kkn_out, a_out, g_out) = refs
    else:
        (x_ref, halo_ref, mu_ref, w0_ref, wup_ref, a0_ref, aup_ref, gup_ref, kk_ref, ka_ref,
         r_out, lw_out, k_out, v_out, kkn_out, a_out, g_out) = refs
    x = x_ref[0]
    tr = x.shape[0]
    width = w0_ref.shape[1]
    halo = jnp.where(pl.program_id(1) > 0, halo_ref[0], 0.0)
    prev = jnp.concatenate([halo, x], axis=0)[SUBLANES - 1:SUBLANES - 1 + tr]
    xs = x + (prev - x) * mu_ref[...]
    o = 3 * width
    n_w, n_a, n_g = wup_ref.shape[0], aup_ref.shape[0], gup_ref.shape[0]
    r, k, v = xs[:, :width], xs[:, width:2 * width], xs[:, 2 * width:o]
    w_lo, a_lo, g_lo = xs[:, o:o + n_w], xs[:, o + n_w:o + n_w + n_a], xs[:, o + n_w + n_a:o + n_w + n_a + n_g]
    if has_vres:
        v_lo = xs[:, o + n_w + n_a + n_g:o + n_w + n_a + n_g + vup_ref.shape[0]]
        v = v + (vf_ref[0] - v) * jax.nn.sigmoid(v0_ref[...] + _dot(v_lo, vup_ref[...]))
    w_log = -_softplus(-(w0_ref[...] + _dot(jnp.tanh(w_lo), wup_ref[...]))) - 0.5
    a = jax.nn.sigmoid(a0_ref[...] + _dot(a_lo, aup_ref[...]))
    kkr = k * kk_ref[...]
    sq = kkr * kkr
    ones = _head_sum_matrix(RW_HEAD_DIM, 1.0)
    for p in range(width // LANES):
        sl = slice(p * LANES, (p + 1) * LANES)
        ss = _dot_exact_rhs(sq[:, sl], ones)
        kkn_out[0, :, sl] = kkr[:, sl] * lax.rsqrt(ss + RW_KK_EPS)
    r_out[0] = r
    lw_out[0] = -jnp.exp(w_log)
    k_out[0] = k * (1.0 + (a - 1.0) * ka_ref[...])
    v_out[0] = v
    a_out[0] = a
    g_out[0] = _dot(jax.nn.sigmoid(g_lo), gup_ref[...])


def rwkv_prep(p, mu, w0, w_up, a0, a_up, g_up, k_k, k_a, v_first, v0, v_up, *, tr=256):
    bsz, t_len, wp = p.shape
    width = w0.shape[-1]
    tr = min(tr, t_len)
    has_vres = v_first is not None
    rows = pl.BlockSpec((1, tr, wp), lambda b, i: (b, i, 0))
    halo = pl.BlockSpec((1, SUBLANES, wp), lambda b, i: (b, jnp.maximum(i * (tr // SUBLANES) - 1, 0), 0))
    full = lambda a: pl.BlockSpec(a.shape, lambda b, i: (0,) * a.ndim)
    seq = pl.BlockSpec((1, tr, width), lambda b, i: (b, i, 0))
    vec = lambda a: a.reshape(1, -1)
    args = [p, p, vec(mu), vec(w0), w_up, vec(a0), a_up, g_up, vec(k_k), vec(k_a)]
    specs = [rows, halo] + [full(a) for a in args[2:]]
    if has_vres:
        extra = [v_first, vec(v0), v_up]
        args += extra
        specs += [seq, full(extra[1]), full(extra[2])]
    return pl.pallas_call(
        functools.partial(_rwkv_prep_kernel, has_vres=has_vres),
        out_shape=tuple(jax.ShapeDtypeStruct((bsz, t_len, width), F32) for _ in range(7)),
        grid=(bsz, t_len // tr),
        in_specs=specs,
        out_specs=tuple(seq for _ in range(7)),
        compiler_params=_params("parallel", "parallel"),
        name="rwkv_prep",
    )(*args)


def _rwkv_post_kernel(y_ref, r_ref, k_ref, v_ref, g_ref, rk_ref, lg_ref, lb_ref, o_ref):
    n = RW_HEAD_DIM
    mean_m = _head_sum_matrix(n, 1.0 / n)
    sum_m = _head_sum_matrix(n, 1.0)
    for p in range(y_ref.shape[2] // LANES):
        sl = slice(p * LANES, (p + 1) * LANES)
        y = y_ref[0, :, sl]
        d = y - _dot_exact_rhs(y, mean_m)
        var = _dot_exact_rhs(d * d, mean_m)
        yn = d * lax.rsqrt(var + RW_LNX_EPS) * lg_ref[:, sl] + lb_ref[:, sl]
        bonus = _dot_exact_rhs(r_ref[0, :, sl] * k_ref[0, :, sl] * rk_ref[:, sl], sum_m) * v_ref[0, :, sl]
        o_ref[0, :, sl] = ((yn + bonus) * g_ref[0, :, sl]).astype(o_ref.dtype)


def rwkv_post(y, r, k, v, g, r_k, lnx_g, lnx_b, *, tr=256):
    bsz, t_len, width = y.shape
    tr = min(tr, t_len)
    seq = pl.BlockSpec((1, tr, width), lambda b, i: (b, i, 0))
    vec = pl.BlockSpec((1, width), lambda b, i: (0, 0))
    return pl.pallas_call(
        _rwkv_post_kernel,
        out_shape=jax.ShapeDtypeStruct((bsz, t_len, width), BF16),
        grid=(bsz, t_len // tr),
        in_specs=[seq] * 5 + [vec] * 3,
        out_specs=seq,
        compiler_params=_params("parallel", "parallel"),
        name="rwkv_post",
    )(y, r, k, v, g, r_k.reshape(1, width), lnx_g.reshape(1, width), lnx_b.reshape(1, width))


def _pad_rows(w, mult):
    return jnp.pad(w, ((0, (-w.shape[0]) % mult), (0, 0)))


def rwkv7_group(xb, w_rw, mu, w0, w_up, a0, a_up, g_up, k_k, k_a, r_k, lnx_g, lnx_b, v_first, v0, v_up,
                bsz, t_len):
    width = w0.shape[-1]
    lora = [w_up.shape[0], a_up.shape[0], g_up.shape[0]] + ([v_up.shape[0]] if v_up is not None else [])
    cols, mus, off = [w_rw[:, :3 * width]], [mu[:3 * width]], 3 * width
    for n in lora:
        cols.append(_pad_cols(w_rw[:, off:off + n], LANES))
        mus.append(jnp.pad(mu[off:off + n], (0, (-n) % LANES)))
        off += n
    w_lay = _pad_cols(jnp.concatenate(cols, axis=1), 256)
    mu_lay = jnp.pad(jnp.concatenate(mus), (0, w_lay.shape[1] - sum(m.shape[0] for m in mus)))
    p = matmul(xb, w_lay.astype(BF16), name="proj_rw").reshape(bsz, t_len, -1)
    up = lambda w: _pad_rows(w, LANES).astype(BF16)
    r, lw, k, v, kk, a, g = rwkv_prep(p, mu_lay, w0, up(w_up), a0, up(a_up), up(g_up), k_k, k_a,
                                      v_first, v0, None if v_up is None else up(v_up))
    if v_first is None:
        v_first = v
    y = rwkv7_recurrence(r, lw, k, v, kk, a)
    return rwkv_post(y, r, k, v, g, r_k.reshape(-1), lnx_g, lnx_b), v_first


def _gdn_prep_kernel(x_ref, halo_ref, w_ref, o_ref, *, mode):
    x = x_ref[0].astype(F32)
    tr = x.shape[0]
    taps = w_ref.shape[0]
    nhalo = halo_ref.shape[1]
    halo = jnp.where(pl.program_id(1) > 0, halo_ref[0].astype(F32), 0.0)
    xc = jnp.concatenate([halo, x], axis=0)
    w = w_ref[...]
    y = sum(w[i:i + 1, :] * xc[nhalo - taps + 1 + i:nhalo - taps + 1 + i + tr] for i in range(taps))
    y = y * jax.nn.sigmoid(y)
    dh = GDN_HEAD_DIM
    for h in range(y.shape[1] // dh):
        seg = y[:, h * dh:(h + 1) * dh]
        if mode != "v":
            seg = seg * lax.rsqrt(jnp.sum(seg * seg, axis=-1, keepdims=True) + GDN_QK_EPS)
        if mode == "q":
            seg = seg * (dh ** -0.5)
        o_ref[0, :, h * dh:(h + 1) * dh] = seg.astype(o_ref.dtype)


def gdn_prep(p, conv_w, col0, ncols, mode, *, tr=512, tw=1024):
    bsz, t_len, _ = p.shape
    tr, tw = min(tr, t_len), min(tw, ncols)
    c0 = col0 // tw
    nhalo = 2 * SUBLANES
    rows = pl.BlockSpec((1, tr, tw), lambda b, i, j: (b, i, c0 + j))
    halo = pl.BlockSpec((1, nhalo, tw), lambda b, i, j: (b, jnp.maximum(i * (tr // nhalo) - 1, 0), c0 + j))
    return pl.pallas_call(
        functools.partial(_gdn_prep_kernel, mode=mode),
        out_shape=jax.ShapeDtypeStruct((bsz, t_len, ncols), BF16),
        grid=(bsz, t_len // tr, ncols // tw),
        in_specs=[rows, halo, pl.BlockSpec((conv_w.shape[0], tw), lambda b, i, j: (0, c0 + j))],
        out_specs=pl.BlockSpec((1, tr, tw), lambda b, i, j: (b, i, j)),
        compiler_params=_params("parallel", "parallel", "parallel"),
        name="gdn_prep_" + mode,
    )(p, p, conv_w)


def gated_deltanet(p_main, ba, conv_w, a_log, dt_bias, norm_w, key_width):
    hv = a_log.shape[0]
    conv_ch = conv_w.shape[1]
    q = gdn_prep(p_main, conv_w, 0, key_width, "q")
    k = gdn_prep(p_main, conv_w, key_width, key_width, "k")
    v = gdn_prep(p_main, conv_w, 2 * key_width, conv_ch - 2 * key_width, "v")
    beta = jax.nn.sigmoid(ba[..., :hv])
    g = -jnp.exp(a_log) * jax.nn.softplus(ba[..., hv:2 * hv] + dt_bias)
    return gdn_recurrence(q, k, v, g, beta, p_main, conv_ch, norm_w)


def _pad_cols(w, mult):
    n = w.shape[1]
    return jnp.pad(w, ((0, 0), (0, (-n) % mult)))


def kernel(x, ev_w_in, ev_shift, ev_w0, ev_w_up, ev_a0, ev_a_up, ev_g_up, ev_k_k, ev_k_a, ev_r_k, ev_lnx_g, ev_lnx_b, vres_w_down, vres_shift, vres_v0, vres_v_up, ev_w_out, od_w_in, od_conv, od_a_log, od_dt_bias, od_norm_w, od_w_out, ln1_g, ln1_b, mlp_w1, mlp_w2, ln2_g, ln2_b):
    bsz, t_len, d_model = x.shape
    depth = ln1_g.shape[0]
    alpha = (2 * depth) ** 0.25
    m = bsz * t_len
    rw_width = ev_w0.shape[1]
    sb_width = ev_w_out.shape[1] - rw_width
    sb_heads = sb_width // SB_HEAD_DIM
    sb_cols = 3 * sb_width
    gdn_hv = od_a_log.shape[1]
    val_width = gdn_hv * GDN_HEAD_DIM
    conv_ch = od_conv.shape[2]
    key_width = (conv_ch - val_width) // 2

    x = x.reshape(m, d_model)
    xb = x.astype(BF16)
    v_first = None
    for layer in range(depth):
        if layer % 2 == 0:
            e = layer // 2
            w_sb = ev_w_in[e][:, :sb_cols]
            w_rw = ev_w_in[e][:, sb_cols:]
            mu = ev_shift[e]
            v0 = v_up = None
            if e > 0:
                w_rw = jnp.concatenate([w_rw, vres_w_down[e - 1]], axis=1)
                mu = jnp.concatenate([mu, vres_shift[e - 1]])
                v0, v_up = vres_v0[e - 1], vres_v_up[e - 1]
            p_sb = matmul(xb, w_sb.astype(BF16), out_dtype=BF16, name="proj_sb")
            o_sb = sb_attention(p_sb.reshape(bsz, t_len, sb_cols), sb_heads)
            o_rw, v_first = rwkv7_group(xb, w_rw, mu, ev_w0[e], ev_w_up[e], ev_a0[e], ev_a_up[e], ev_g_up[e],
                                        ev_k_k[e], ev_k_a[e], ev_r_k[e], ev_lnx_g[e], ev_lnx_b[e],
                                        v_first, v0, v_up, bsz, t_len)
            x, xb = matmul_ln([o_sb.reshape(m, -1), o_rw.reshape(m, -1)], ev_w_out[e].astype(BF16), x,
                              ln1_g[layer], ln1_b[layer], alpha, tm=256, name="proj_out_ln")
        else:
            o = layer // 2
            w_main = od_w_in[o][:, :conv_ch + val_width]
            w_ba = od_w_in[o][:, conv_ch + val_width:]
            p_main = matmul(xb, w_main.astype(BF16), out_dtype=BF16, name="proj_gdn").reshape(bsz, t_len, -1)
            p_ba = matmul(xb, _pad_cols(w_ba, LANES).astype(BF16), name="proj_gdn_gates")[:, :2 * gdn_hv]
            mixed = gated_deltanet(p_main, p_ba.reshape(bsz, t_len, -1), od_conv[o], od_a_log[o],
                                   od_dt_bias[o], od_norm_w[o], key_width)
            x, xb = matmul_ln([mixed.reshape(m, -1)], od_w_out[o].astype(BF16), x, ln1_g[layer], ln1_b[layer],
                              alpha, tm=512, tk=1024, name="proj_out_ln")
        hid = matmul(xb, mlp_w1[layer].astype(BF16), act="relu2", out_dtype=BF16, name="mlp_up")
        x, xb = matmul_ln([hid], mlp_w2[layer].astype(BF16), x, ln2_g[layer], ln2_b[layer], alpha,
                          tm=512, tk=1024, name="mlp_down_ln")
    return x.reshape(bsz, t_len, d_model)
```

```python
import functools
import math

import jax
import jax.numpy as jnp
from jax import lax
from jax.experimental import pallas as pl
from jax.experimental.pallas import tpu as pltpu

F32 = jnp.float32
BF16 = jnp.bfloat16

LANES = 128
SUBLANES = 8
CHUNK = 64
VMEM_LIMIT = 48 * 1024 * 1024

SB_HEAD_DIM = 128
RW_HEAD_DIM = 64
GDN_HEAD_DIM = 128
GDN_CONV = 4
RW_LNX_EPS = 64e-5
RW_KK_EPS = 1e-12
GDN_NORM_EPS = 1e-6
GDN_QK_EPS = 1e-6
LN_EPS = 1e-5


def _params(*sem):
    return pltpu.CompilerParams(dimension_semantics=sem, vmem_limit_bytes=VMEM_LIMIT)


def _dot(a, b):
    return jnp.dot(a.astype(BF16), b.astype(BF16), preferred_element_type=F32)


def _dot_nt(a, b):
    return lax.dot_general(a.astype(BF16), b.astype(BF16), (((1,), (1,)), ((), ())),
                           preferred_element_type=F32)


def _dot_tn(a, b):
    return lax.dot_general(a.astype(BF16), b.astype(BF16), (((0,), (0,)), ((), ())),
                           preferred_element_type=F32)


def _split3(x):
    x1 = x.astype(BF16)
    r1 = x - x1.astype(F32)
    x2 = r1.astype(BF16)
    x3 = (r1 - x2.astype(F32)).astype(BF16)
    return x1, x2, x3


def _dot_exact_lhs(m01, x):
    x1, x2, x3 = _split3(x)
    d = lambda t: jnp.dot(m01, t, preferred_element_type=F32)
    return d(x1) + d(x2) + d(x3)


def _dot_exact_rhs(x, m01):
    x1, x2, x3 = _split3(x)
    d = lambda t: jnp.dot(t, m01, preferred_element_type=F32)
    return d(x1) + d(x2) + d(x3)


def _tri(n, kind):
    r = lax.broadcasted_iota(jnp.int32, (n, n), 0)
    c = lax.broadcasted_iota(jnp.int32, (n, n), 1)
    return {"ge": r >= c, "gt": r > c, "le": r <= c}[kind]


def _mm_kernel(a_ref, w_ref, o_ref, *scratch, act, nk):
    prod = jnp.dot(a_ref[...], w_ref[...], preferred_element_type=F32)

    def finish(acc):
        if act == "relu2":
            acc = jnp.square(jnp.maximum(acc, 0.0))
        o_ref[...] = acc.astype(o_ref.dtype)

    if nk == 1:
        finish(prod)
    else:
        acc_ref, = scratch
        k = pl.program_id(2)

        @pl.when(k == 0)
        def _():
            acc_ref[...] = prod

        @pl.when(jnp.logical_and(k > 0, k < nk - 1))
        def _():
            acc_ref[...] += prod

        @pl.when(k == nk - 1)
        def _():
            finish(acc_ref[...] + prod)


def _pick(n, cands):
    for c in cands:
        if n % c == 0:
            return c
    raise ValueError(f"no tile for {n}")


def matmul(a, w, *, act=None, out_dtype=F32, name="mm"):
    m, k = a.shape
    k2, n = w.shape
    assert k == k2
    tm = _pick(m, (1024, 512, 256, 128, 64, 32, 16, 8))
    tn = _pick(n, (1024, 512, 256, 128))
    tk = _pick(k, (2048, 1024, 512, 256, 128))
    nk = k // tk
    scratch = [pltpu.VMEM((tm, tn), F32)] if nk > 1 else []
    return pl.pallas_call(
        functools.partial(_mm_kernel, act=act, nk=nk),
        out_shape=jax.ShapeDtypeStruct((m, n), out_dtype),
        grid=(m // tm, n // tn, nk),
        in_specs=[pl.BlockSpec((tm, tk), lambda i, j, kk: (i, kk)),
                  pl.BlockSpec((tk, tn), lambda i, j, kk: (kk, j))],
        out_specs=pl.BlockSpec((tm, tn), lambda i, j, kk: (i, j)),
        scratch_shapes=scratch,
        compiler_params=_params("parallel", "parallel", "arbitrary"),
        name=name,
    )(a, w)


def _residual_ln(x, h, g, b, alpha, o_ref, ob_ref):
    y = alpha * x + h
    mu = jnp.mean(y, axis=-1, keepdims=True)
    yc = y - mu
    var = jnp.mean(jnp.square(yc), axis=-1, keepdims=True)
    out = yc * lax.rsqrt(var + LN_EPS) * g + b
    o_ref[...] = out
    ob_ref[...] = out.astype(BF16)


def _mm_ln_kernel(*refs, alpha, n_lhs, nk):
    a_refs, w_refs = refs[:n_lhs], refs[n_lhs:2 * n_lhs]
    x_ref, g_ref, b_ref, o_ref, ob_ref = refs[2 * n_lhs:2 * n_lhs + 5]
    prod = sum(jnp.dot(a[...], w[...], preferred_element_type=F32) for a, w in zip(a_refs, w_refs))
    if nk == 1:
        _residual_ln(x_ref[...], prod, g_ref[...], b_ref[...], alpha, o_ref, ob_ref)
    else:
        acc_ref = refs[-1]
        k = pl.program_id(1)

        @pl.when(k == 0)
        def _():
            acc_ref[...] = prod

        @pl.when(jnp.logical_and(k > 0, k < nk - 1))
        def _():
            acc_ref[...] += prod

        @pl.when(k == nk - 1)
        def _():
            _residual_ln(x_ref[...], acc_ref[...] + prod, g_ref[...], b_ref[...], alpha, o_ref, ob_ref)


def matmul_ln(lhs, w, x, g, b, alpha, *, tm, tk=None, name="mm_ln"):
    m, d = x.shape
    n_lhs = len(lhs)
    kw = lhs[0].shape[1]
    assert all(a.shape == (m, kw) for a in lhs) and w.shape == (n_lhs * kw, d)
    tm = min(tm, m)
    if tk is None:
        nk, tk = 1, kw
        a_specs = [pl.BlockSpec((tm, kw), lambda i, kk: (i, 0)) for _ in lhs]
        w_specs = [pl.BlockSpec((kw, d), lambda i, kk, j=j: (j, 0)) for j in range(n_lhs)]
    else:
        assert n_lhs == 1 and kw % tk == 0
        nk = kw // tk
        a_specs = [pl.BlockSpec((tm, tk), lambda i, kk: (i, kk))]
        w_specs = [pl.BlockSpec((tk, d), lambda i, kk: (kk, 0))]
    row = pl.BlockSpec((tm, d), lambda i, kk: (i, 0))
    vec = pl.BlockSpec((1, d), lambda i, kk: (0, 0))
    return pl.pallas_call(
        functools.partial(_mm_ln_kernel, alpha=alpha, n_lhs=n_lhs, nk=nk),
        out_shape=(jax.ShapeDtypeStruct((m, d), F32), jax.ShapeDtypeStruct((m, d), BF16)),
        grid=(m // tm, nk),
        in_specs=a_specs + w_specs + [row, vec, vec],
        out_specs=(row, row),
        scratch_shapes=[pltpu.VMEM((tm, d), F32)] if nk > 1 else [],
        compiler_params=_params("parallel", "arbitrary"),
        name=name,
    )(*lhs, *([w] * n_lhs), x, g.reshape(1, d), b.reshape(1, d))


def _split_hi_lo(x):
    hi = lax.bitcast_convert_type(lax.bitcast_convert_type(x, jnp.uint32) & jnp.uint32(0xFFFF0000), F32)
    return hi.astype(BF16), (x - hi).astype(BF16)


def _sb_kernel(q_ref, k_ref, v_ref, o_ref, *, tq, scale, nh):
    qi = pl.program_id(2)
    d = SB_HEAD_DIM
    tk = 2 * tq
    r = lax.broadcasted_iota(jnp.int32, (tq, tq), 0)
    c = lax.broadcasted_iota(jnp.int32, (tq, tq), 1)
    suffix = (r >= c).astype(BF16)
    suffix2 = jnp.concatenate([suffix, suffix], axis=0)
    qs = [q_ref[0, :, h * d:(h + 1) * d] for h in range(nh)]
    streams = [(h, half) for h in range(nh) for half in (1, 0)]

    def rows(j, half):
        return pl.ds(pl.multiple_of(j * tk + half * tq, tq), tq)

    def cumsum(z, keep):
        sp = jnp.maximum(z, 0.0) + jnp.log(1.0 + jnp.exp(-jnp.abs(z)))
        if keep is not None:
            sp = jnp.where(keep, sp, 0.0)
        hi, lo = _split_hi_lo(sp)
        return jnp.dot(jnp.concatenate([hi, lo], axis=1), suffix2, preferred_element_type=F32)

    def block(j, carry, diagonal):
        accs, cins = list(carry[0]), list(carry[1])
        zs = [lax.dot_general(qs[h], k_ref[0, rows(j, half), h * d:(h + 1) * d], (((1,), (1,)), ((), ())),
                              preferred_element_type=F32) * scale for h, half in streams]
        keeps = [(j * tk + half * tq + c < qi * tq + r) if diagonal else None for _, half in streams]
        css = [cumsum(z, keep) for z, keep in zip(zs, keeps)]
        for (h, half), z, cs, keep in zip(streams, zs, css, keeps):
            a = jnp.exp(z - cs - cins[h])
            if keep is not None:
                a = jnp.where(keep, a, 0.0)
            accs[h] = accs[h] + jnp.dot(a.astype(BF16), v_ref[0, rows(j, half), h * d:(h + 1) * d],
                                        preferred_element_type=F32)
            cins[h] = cins[h] + cs[:, :1]
        return tuple(accs), tuple(cins)

    nfull = qi // 2
    carry = (tuple(jnp.zeros((tq, d), F32) for _ in range(nh)), tuple(jnp.zeros((tq, 1), F32) for _ in range(nh)))
    carry = block(nfull, carry, True)
    accs, _ = lax.fori_loop(0, nfull, lambda i, cr: block(nfull - 1 - i, cr, False), carry)
    for h in range(nh):
        o_ref[0, :, h * d:(h + 1) * d] = accs[h].astype(o_ref.dtype)


def sb_attention(p, n_heads, *, tq=256, nh=2):
    bsz, t_len, _ = p.shape
    d = SB_HEAD_DIM
    assert t_len % (2 * tq) == 0 and n_heads % nh == 0
    ng = n_heads // nh
    w = nh * d
    return pl.pallas_call(
        functools.partial(_sb_kernel, tq=tq, scale=d ** -0.5, nh=nh),
        out_shape=jax.ShapeDtypeStruct((bsz, t_len, n_heads * d), BF16),
        grid=(bsz, ng, t_len // tq),
        in_specs=[pl.BlockSpec((1, tq, w), lambda b, h, i: (b, i, h)),
                  pl.BlockSpec((1, t_len, w), lambda b, h, i: (b, 0, ng + h)),
                  pl.BlockSpec((1, t_len, w), lambda b, h, i: (b, 0, 2 * ng + h))],
        out_specs=pl.BlockSpec((1, tq, w), lambda b, h, i: (b, i, h)),
        compiler_params=_params("parallel", "parallel", "arbitrary"),
        name="sb_attention",
    )(p, p, p)


def _tri_inv_kernel(l_ref, t_ref, lt_ref, tt_ref, *, c):
    for t in range(c):
        lt_ref[t * c:(t + 1) * c, :] = l_ref[:, t, :].T
    nb = c // SUBLANES
    sub = lax.broadcasted_iota(jnp.int32, (SUBLANES, LANES), 0)
    zero = jnp.zeros((SUBLANES, LANES), F32)
    for t in range(c):
        tb = t // SUBLANES
        accs = [zero] * (tb + 1)
        accs[tb] = jnp.where(sub == (t % SUBLANES), 1.0, 0.0)
        for j in range(t):
            lt = jnp.broadcast_to(lt_ref[t * c + j:t * c + j + 1, :], (SUBLANES, LANES))
            for cb in range(j // SUBLANES + 1):
                accs[cb] = accs[cb] - lt * tt_ref[j * c + cb * SUBLANES:j * c + (cb + 1) * SUBLANES, :]
        for cb in range(nb):
            tt_ref[t * c + cb * SUBLANES:t * c + (cb + 1) * SUBLANES, :] = accs[cb] if cb <= tb else zero
    for t in range(c):
        t_ref[:, t, :] = tt_ref[t * c:(t + 1) * c, :].T


def tri_inverse(l):
    shape = l.shape
    c = shape[-1]
    n_sys = math.prod(shape[:-2])
    n = -(-n_sys // LANES) * LANES
    flat = l.reshape(n_sys, c, c)
    if n != n_sys:
        flat = jnp.pad(flat, ((0, n - n_sys), (0, 0), (0, 0)))
    blk = pl.BlockSpec((LANES, c, c), lambda i: (i, 0, 0))
    out = pl.pallas_call(
        functools.partial(_tri_inv_kernel, c=c),
        out_shape=jax.ShapeDtypeStruct((n, c, c), F32),
        grid=(n // LANES,),
        in_specs=[blk],
        out_specs=blk,
        scratch_shapes=[pltpu.VMEM((c * c, LANES), F32), pltpu.VMEM((c * c, LANES), F32)],
        compiler_params=_params("parallel"),
        name="tri_inverse",
    )(flat)
    return (out if n == n_sys else out[:n_sys]).reshape(shape)


def _bdot(a, b):
    return jnp.dot(a, b, preferred_element_type=F32)


def _rwkv_decayed(lw, kk, a):
    c = lw.shape[0]
    g = _dot_exact_lhs(_tri(c, "ge").astype(BF16), lw)
    b = a * kk
    return g, b, kk * jnp.exp(g - lw), b * jnp.exp(-g)


def _rwkv_l_kernel(lw_ref, kk_ref, a_ref, l_ref):
    c = lw_ref.shape[1]
    nh = l_ref.shape[2]
    n = RW_HEAD_DIM
    _, _, kq, bk = _rwkv_decayed(lw_ref[0], kk_ref[0], a_ref[0])
    strict = _tri(c, "gt")
    lane = lax.broadcasted_iota(jnp.int32, (c, LANES), 1)
    sel = [lane < n, lane >= n]
    kqb, bkb = kq.astype(BF16), bk.astype(BF16)
    lhs = [jnp.where(sel[h % 2], kqb[:, (h // 2) * LANES:(h // 2 + 1) * LANES], 0) for h in range(nh)]
    prods = [_dot_nt(lhs[h], bkb[:, (h // 2) * LANES:(h // 2 + 1) * LANES]) for h in range(nh)]
    for h in range(nh):
        l_ref[0, 0, h] = jnp.where(strict, prods[h], 0.0)


def _rwkv_chunk_kernel(r_ref, lw_ref, k_ref, v_ref, kk_ref, a_ref, t_ref, y_ref, s_ref):
    c = lw_ref.shape[1]
    nh = t_ref.shape[2]
    n = RW_HEAD_DIM
    npair = nh // 2

    @pl.when(pl.program_id(1) == 0)
    def _():
        s_ref[...] = jnp.zeros_like(s_ref)

    r, lw, k, v, kk, a = (ref[0] for ref in (r_ref, lw_ref, k_ref, v_ref, kk_ref, a_ref))
    g, b, kq, bk = _rwkv_decayed(lw, kk, a)
    eg = jnp.exp(g)
    rq = r * eg
    kh = k * jnp.exp(-g)
    glast = g[c - 1:c, :]
    tail = jnp.exp(glast - g)
    kd = (k * tail).astype(BF16)
    bd = (b * tail).astype(BF16)
    pc = jnp.exp(glast)
    kqb, rqb, khb, bkb, vb = (x.astype(BF16) for x in (kq, rq, kh, bk, v))

    lane = lax.broadcasted_iota(jnp.int32, (c, LANES), 1)
    row = lax.broadcasted_iota(jnp.int32, (c, LANES), 0)
    lo_half = lane < n
    sel = [lo_half, jnp.logical_not(lo_half)]
    col = jnp.where(lo_half, lane, lane - n)
    m_uk = jnp.logical_and(lo_half, row > col)
    m_y = row >= col
    m_yb = jnp.logical_and(m_y, jnp.logical_not(lo_half))
    zeros = jnp.zeros((c, LANES), BF16)
    pl_ = lambda x, p: x[:, p * LANES:(p + 1) * LANES]

    heads = range(nh)
    lhs = [jnp.concatenate([jnp.where(sel[h % 2], pl_(kqb, h // 2), 0), jnp.where(sel[h % 2], pl_(rqb, h // 2), 0)],
                           axis=0) for h in heads]
    rhs = [jnp.concatenate([pl_(khb, p), pl_(bkb, p)], axis=0) for p in range(npair)]
    gm = [_dot_nt(lhs[h], rhs[h // 2]) for h in heads]
    a_u = [jnp.where(m_uk, gm[h][:c], 0.0).astype(BF16) for h in heads]
    a_y = [jnp.where(m_y, gm[h][c:], 0.0) for h in heads]
    a_ys = [jnp.where(lo_half, a_y[h], -a_y[h]).astype(BF16) for h in heads]
    a_yb = [jnp.where(m_yb, gm[h][c:], 0.0).astype(BF16) for h in heads]
    vz = [jnp.concatenate([pl_(vb, p), zeros], axis=0) for p in range(npair)]
    x1 = [_bdot(a_u[h], vz[h // 2]).astype(BF16) for h in heads]
    tb = [t_ref[0, 0, h].astype(BF16) for h in heads]
    tw = [_bdot(tb[h], jnp.concatenate([pl_(kqb, h // 2), x1[h]], axis=1)).astype(BF16) for h in heads]
    qe = [pl_(rq, h // 2) - _bdot(a_yb[h], jnp.concatenate([zeros, tw[h][:, :LANES]], axis=0)) for h in heads]
    yl = [_bdot(a_ys[h], jnp.concatenate([pl_(vb, h // 2), tw[h][:, LANES:]], axis=0)) for h in heads]
    r2 = lax.broadcasted_iota(jnp.int32, (LANES, LANES), 0)
    c2 = lax.broadcasted_iota(jnp.int32, (LANES, LANES), 1)
    bdiag = (r2 < n) == (c2 < n)
    pairs = range(npair)
    merge = lambda xs, p: jnp.where(lo_half, xs[2 * p], xs[2 * p + 1])
    wq = [merge([t[:, :LANES] for t in tw], p) for p in pairs]
    ut = [merge([t[:, LANES:] for t in tw], p) for p in pairs]
    qeff = [merge(qe, p).astype(BF16) for p in pairs]
    mc = [jnp.where(bdiag, _dot_tn(wq[p], pl_(bd, p)), 0.0).astype(BF16) for p in pairs]
    nn = [jnp.where(bdiag, _dot_tn(pl_(vb, p), pl_(kd, p)) - _dot_tn(ut[p], pl_(bd, p)), 0.0) for p in pairs]
    s = [s_ref[p] for p in pairs]
    sb = [x.astype(BF16) for x in s]
    y = [_dot_nt(qeff[p], sb[p]) for p in pairs]
    sm = [_bdot(sb[p], mc[p]) for p in pairs]
    for p in pairs:
        y_ref[0, :, p * LANES:(p + 1) * LANES] = y[p] + merge(yl, p)
        s_ref[p] = s[p] * pl_(pc, p) - sm[p] + nn[p]


def rwkv7_recurrence(r, lw, k, v, kk, a):
    bsz, t_len, width = r.shape
    nh = width // RW_HEAD_DIM
    npair = nh // 2
    c = min(CHUNK, t_len)
    nc = t_len // c
    seq = pl.BlockSpec((1, c, width), lambda b, i: (b, i, 0))
    mat = pl.BlockSpec((1, 1, nh, c, c), lambda b, i: (b, i, 0, 0, 0))
    l = pl.pallas_call(
        _rwkv_l_kernel,
        out_shape=jax.ShapeDtypeStruct((bsz, nc, nh, c, c), F32),
        grid=(bsz, nc),
        in_specs=[seq, seq, seq],
        out_specs=mat,
        compiler_params=_params("parallel", "parallel"),
        name="rwkv_l",
    )(lw, kk, a)
    tinv = tri_inverse(l)
    return pl.pallas_call(
        _rwkv_chunk_kernel,
        out_shape=jax.ShapeDtypeStruct((bsz, t_len, width), F32),
        grid=(bsz, nc),
        in_specs=[seq] * 6 + [mat],
        out_specs=seq,
        scratch_shapes=[pltpu.VMEM((npair, LANES, LANES), F32)],
        compiler_params=_params("parallel", "arbitrary"),
        name="rwkv_chunk",
    )(r, lw, k, v, kk, a, tinv)


def _gdn_l_kernel(k_ref, beta_ref, gcol_ref, grow_ref, l_ref, *, hv, rep):
    c = k_ref.shape[1]
    dh = GDN_HEAD_DIM
    incl = _tri(c, "ge")
    strict = _tri(c, "gt")
    gc_col = _dot_exact_lhs(incl.astype(BF16), gcol_ref[0, 0, 0])
    gc_row = _dot_exact_rhs(grow_ref[0, 0, 0], _tri(c, "le").astype(BF16))
    beta_all = beta_ref[0, 0, 0]
    ks = [k_ref[0, :, i * dh:(i + 1) * dh] for i in range(hv // rep)]
    kkt = [_dot_nt(x, x) for x in ks]
    decay = [jnp.exp(jnp.minimum(gc_col[:, j:j + 1] - gc_row[j:j + 1, :], 0.0)) for j in range(hv)]
    for j in range(hv):
        l_ref[0, 0, j] = jnp.where(strict, kkt[j // rep] * beta_all[:, j:j + 1] * decay[j], 0.0)


def _gdn_chunk_kernel(q_ref, k_ref, v_ref, beta_ref, gcol_ref, grow_ref, t_ref, z_ref, nw_ref, o_ref, s_ref,
                      *, hv, rep):
    c = k_ref.shape[1]
    dh = GDN_HEAD_DIM

    @pl.when(pl.program_id(2) == 0)
    def _():
        s_ref[...] = jnp.zeros_like(s_ref)

    incl = _tri(c, "ge")
    gc_col = _dot_exact_lhs(incl.astype(BF16), gcol_ref[0, 0, 0])
    gc_row = _dot_exact_rhs(grow_ref[0, 0, 0], _tri(c, "le").astype(BF16))
    beta_all = beta_ref[0, 0, 0]
    hk = hv // rep
    ks = [k_ref[0, :, i * dh:(i + 1) * dh] for i in range(hk)]
    qs = [q_ref[0, :, i * dh:(i + 1) * dh] for i in range(hk)]
    qk = [_dot_nt(qs[i], ks[i]) for i in range(hk)]
    heads = range(hv)
    gcc = [gc_col[:, j:j + 1] for j in heads]
    decay = [jnp.where(incl, jnp.exp(jnp.minimum(gcc[j] - gc_row[j:j + 1, :], 0.0)), 0.0) for j in heads]
    egc = [jnp.exp(gcc[j]) for j in heads]
    glast = [gcc[j][c - 1:c, :] for j in heads]
    beta = [beta_all[:, j:j + 1] for j in heads]
    attn = [(qk[j // rep] * decay[j]).astype(BF16) for j in heads]
    kd = [(ks[j // rep] * jnp.exp(glast[j] - gcc[j])).astype(BF16) for j in heads]
    rhs = [jnp.concatenate([v_ref[0, :, j * dh:(j + 1) * dh] * beta[j],
                            ks[j // rep] * (beta[j] * egc[j])], axis=1).astype(BF16) for j in heads]
    uw = [jnp.dot(t_ref[0, 0, j].astype(BF16), rhs[j], preferred_element_type=F32).astype(BF16) for j in heads]
    us = [x[:, :dh] for x in uw]
    ws = [x[:, dh:] for x in uw]
    qeff = [(qs[j // rep] * egc[j] - jnp.dot(attn[j], ws[j], preferred_element_type=F32)).astype(BF16)
            for j in heads]
    ol = [jnp.dot(attn[j], us[j], preferred_element_type=F32) for j in heads]
    mc = [_dot_tn(kd[j], ws[j]).astype(BF16) for j in heads]
    nn = [_dot_tn(kd[j], us[j]) for j in heads]
    s = [s_ref[j] for j in heads]
    sb = [x.astype(BF16) for x in s]
    o = [jnp.dot(qeff[j], sb[j], preferred_element_type=F32) for j in heads]
    ms = [jnp.dot(mc[j], sb[j], preferred_element_type=F32) for j in heads]
    for j in heads:
        sl = slice(j * dh, (j + 1) * dh)
        oj = o[j] + ol[j]
        oj = oj * lax.rsqrt(jnp.mean(oj * oj, axis=-1, keepdims=True) + GDN_NORM_EPS) * nw_ref[...]
        z = z_ref[0, :, sl].astype(F32)
        o_ref[0, :, sl] = (oj * (z * jax.nn.sigmoid(z))).astype(o_ref.dtype)
        s_ref[j] = s[j] * jnp.exp(glast[j]) - ms[j] + nn[j]


def gdn_recurrence(q, k, v, g, beta, p_main, z_col0, norm_w, *, hv_group=16):
    bsz, t_len, kw = q.shape
    dh = GDN_HEAD_DIM
    hk = kw // dh
    hv = v.shape[-1] // dh
    rep = hv // hk
    hvg = min(hv_group, hv)
    ng = hv // hvg
    c = min(CHUNK, t_len)
    nc = t_len // c
    col = lambda x: x.reshape(bsz, nc, c, ng, hvg).transpose(0, 3, 1, 2, 4)
    g_col, beta_col = col(g), col(beta)
    g_row = g_col.transpose(0, 1, 2, 4, 3)
    qk_spec = pl.BlockSpec((1, c, hvg // rep * dh), lambda b, h, i: (b, i, h))
    v_spec = pl.BlockSpec((1, c, hvg * dh), lambda b, h, i: (b, i, h))
    col_spec = pl.BlockSpec((1, 1, 1, c, hvg), lambda b, h, i: (b, h, i, 0, 0))
    row_spec = pl.BlockSpec((1, 1, 1, hvg, c), lambda b, h, i: (b, h, i, 0, 0))
    mat = pl.BlockSpec((1, 1, hvg, c, c), lambda b, h, i: (b, i, h, 0, 0))
    l = pl.pallas_call(
        functools.partial(_gdn_l_kernel, hv=hvg, rep=rep),
        out_shape=jax.ShapeDtypeStruct((bsz, nc, hv, c, c), F32),
        grid=(bsz, ng, nc),
        in_specs=[qk_spec, col_spec, col_spec, row_spec],
        out_specs=mat,
        compiler_params=_params("parallel", "parallel", "parallel"),
        name="gdn_l",
    )(k, beta_col, g_col, g_row)
    tinv = tri_inverse(l)
    z0 = z_col0 // (hvg * dh)
    z_spec = pl.BlockSpec((1, c, hvg * dh), lambda b, h, i: (b, i, z0 + h))
    return pl.pallas_call(
        functools.partial(_gdn_chunk_kernel, hv=hvg, rep=rep),
        out_shape=jax.ShapeDtypeStruct((bsz, t_len, hv * dh), BF16),
        grid=(bsz, ng, nc),
        in_specs=[qk_spec, qk_spec, v_spec, col_spec, col_spec, row_spec, mat, z_spec,
                  pl.BlockSpec((1, dh), lambda b, h, i: (0, 0))],
        out_specs=v_spec,
        scratch_shapes=[pltpu.VMEM((hvg, dh, dh), F32)],
        compiler_params=_params("parallel", "parallel", "arbitrary"),
        name="gdn_chunk",
    )(q, k, v, beta_col, g_col, g_row, tinv, p_main, norm_w.reshape(1, dh))


def _heads(t, hd):
    return t.reshape(t.shape[:-1] + (t.shape[-1] // hd, hd))


def _l2n(x, eps):
    return x * lax.rsqrt(jnp.sum(x * x, axis=-1, keepdims=True) + eps)


def _softplus(x):
    return jnp.maximum(x, 0.0) + jnp.log(1.0 + jnp.exp(-jnp.abs(x)))


def _head_sum_matrix(n, value):
    r = lax.broadcasted_iota(jnp.int32, (LANES, LANES), 0) // n
    c = lax.broadcasted_iota(jnp.int32, (LANES, LANES), 1) // n
    return jnp.where(r == c, value, 0.0).astype(BF16)


def _rwkv_prep_kernel(*refs, has_vres):
    if has_vres:
        (x_ref, halo_ref, mu_ref, w0_ref, wup_ref, a0_ref, aup_ref, gup_ref, kk_ref, ka_ref,
         vf_ref, v0_ref, vup_ref, r_out, lw_out, k_out, v_out, kkn_out, a_out, g_out) = refs
    else:
        (x_ref, halo_ref, mu_ref, w0_ref, wup_ref, a0_ref, aup_ref, gup_ref, kk_ref, ka_ref,
         r_out, lw_out, k_out, v_out, kkn_out, a_out, g_out) = refs
    x = x_ref[0]
    tr = x.shape[0]
    width = w0_ref.shape[1]
    halo = jnp.where(pl.program_id(1) > 0, halo_ref[0], 0.0)
    prev = jnp.concatenate([halo, x], axis=0)[SUBLANES - 1:SUBLANES - 1 + tr]
    xs = x + (prev - x) * mu_ref[...]
    o = 3 * width
    n_w, n_a, n_g = wup_ref.shape[0], aup_ref.shape[0], gup_ref.shape[0]
    r, k, v = xs[:, :width], xs[:, width:2 * width], xs[:, 2 * width:o]
    w_lo, a_lo, g_lo = xs[:, o:o + n_w], xs[:, o + n_w:o + n_w + n_a], xs[:, o + n_w + n_a:o + n_w + n_a + n_g]
    if has_vres:
        v_lo = xs[:, o + n_w + n_a + n_g:o + n_w + n_a + n_g + vup_ref.shape[0]]
        v = v + (vf_ref[0] - v) * jax.nn.sigmoid(v0_ref[...] + _dot(v_lo, vup_ref[...]))
    w_log = -_softplus(-(w0_ref[...] + _dot(jnp.tanh(w_lo), wup_ref[...]))) - 0.5
    a = jax.nn.sigmoid(a0_ref[...] + _dot(a_lo, aup_ref[...]))
    kkr = k * kk_ref[...]
    sq = kkr * kkr
    ones = _head_sum_matrix(RW_HEAD_DIM, 1.0)
    for p in range(width // LANES):
        sl = slice(p * LANES, (p + 1) * LANES)
        ss = _dot_exact_rhs(sq[:, sl], ones)
        kkn_out[0, :, sl] = kkr[:, sl] * lax.rsqrt(ss + RW_KK_EPS)
    r_out[0] = r
    lw_out[0] = -jnp.exp(w_log)
    k_out[0] = k * (1.0 + (a - 1.0) * ka_ref[...])
    v_out[0] = v
    a_out[0] = a
    g_out[0] = _dot(jax.nn.sigmoid(g_lo), gup_ref[...])


def rwkv_prep(p, mu, w0, w_up, a0, a_up, g_up, k_k, k_a, v_first, v0, v_up, *, tr=256):
    bsz, t_len, wp = p.shape
    width = w0.shape[-1]
    tr = min(tr, t_len)
    has_vres = v_first is not None
    rows = pl.BlockSpec((1, tr, wp), lambda b, i: (b, i, 0))
    halo = pl.BlockSpec((1, SUBLANES, wp), lambda b, i: (b, jnp.maximum(i * (tr // SUBLANES) - 1, 0), 0))
    full = lambda a: pl.BlockSpec(a.shape, lambda b, i: (0,) * a.ndim)
    seq = pl.BlockSpec((1, tr, width), lambda b, i: (b, i, 0))
    vec = lambda a: a.reshape(1, -1)
    args = [p, p, vec(mu), vec(w0), w_up, vec(a0), a_up, g_up, vec(k_k), vec(k_a)]
    specs = [rows, halo] + [full(a) for a in args[2:]]
    if has_vres:
        extra = [v_first, vec(v0), v_up]
        args += extra
        specs += [seq, full(extra[1]), full(extra[2])]
    return pl.pallas_call(
        functools.partial(_rwkv_prep_kernel, has_vres=has_vres),
        out_shape=tuple(jax.ShapeDtypeStruct((bsz, t_len, width), F32) for _ in range(7)),
        grid=(bsz, t_len // tr),
        in_specs=specs,
        out_specs=tuple(seq for _ in range(7)),
        compiler_params=_params("parallel", "parallel"),
        name="rwkv_prep",
    )(*args)


def _rwkv_post_kernel(y_ref, r_ref, k_ref, v_ref, g_ref, rk_ref, lg_ref, lb_ref, o_ref):
    n = RW_HEAD_DIM
    mean_m = _head_sum_matrix(n, 1.0 / n)
    sum_m = _head_sum_matrix(n, 1.0)
    for p in range(y_ref.shape[2] // LANES):
        sl = slice(p * LANES, (p + 1) * LANES)
        y = y_ref[0, :, sl]
        d = y - _dot_exact_rhs(y, mean_m)
        var = _dot_exact_rhs(d * d, mean_m)
        yn = d * lax.rsqrt(var + RW_LNX_EPS) * lg_ref[:, sl] + lb_ref[:, sl]
        bonus = _dot_exact_rhs(r_ref[0, :, sl] * k_ref[0, :, sl] * rk_ref[:, sl], sum_m) * v_ref[0, :, sl]
        o_ref[0, :, sl] = ((yn + bonus) * g_ref[0, :, sl]).astype(o_ref.dtype)


def rwkv_post(y, r, k, v, g, r_k, lnx_g, lnx_b, *, tr=256):
    bsz, t_len, width = y.shape
    tr = min(tr, t_len)
    seq = pl.BlockSpec((1, tr, width), lambda b, i: (b, i, 0))
    vec = pl.BlockSpec((1, width), lambda b, i: (0, 0))
    return pl.pallas_call(
        _rwkv_post_kernel,
        out_shape=jax.ShapeDtypeStruct((bsz, t_len, width), BF16),
        grid=(bsz, t_len // tr),
        in_specs=[seq] * 5 + [vec] * 3,
        out_specs=seq,
        compiler_params=_params("parallel", "parallel"),
        name="rwkv_post",
    )(y, r, k, v, g, r_k.reshape(1, width), lnx_g.reshape(1, width), lnx_b.reshape(1, width))


def _pad_rows(w, mult):
    return jnp.pad(w, ((0, (-w.shape[0]) % mult), (0, 0)))


def rwkv7_group(xb, w_rw, mu, w0, w_up, a0, a_up, g_up, k_k, k_a, r_k, lnx_g, lnx_b, v_first, v0, v_up,
                bsz, t_len):
    width = w0.shape[-1]
    lora = [w_up.shape[0], a_up.shape[0], g_up.shape[0]] + ([v_up.shape[0]] if v_up is not None else [])
    cols, mus, off = [w_rw[:, :3 * width]], [mu[:3 * width]], 3 * width
    for n in lora:
        cols.append(_pad_cols(w_rw[:, off:off + n], LANES))
        mus.append(jnp.pad(mu[off:off + n], (0, (-n) % LANES)))
        off += n
    w_lay = _pad_cols(jnp.concatenate(cols, axis=1), 256)
    mu_lay = jnp.pad(jnp.concatenate(mus), (0, w_lay.shape[1] - sum(m.shape[0] for m in mus)))
    p = matmul(xb, w_lay.astype(BF16), name="proj_rw").reshape(bsz, t_len, -1)
    up = lambda w: _pad_rows(w, LANES).astype(BF16)
    r, lw, k, v, kk, a, g = rwkv_prep(p, mu_lay, w0, up(w_up), a0, up(a_up), up(g_up), k_k, k_a,
                                      v_first, v0, None if v_up is None else up(v_up))
    if v_first is None:
        v_first = v
    y = rwkv7_recurrence(r, lw, k, v, kk, a)
    return rwkv_post(y, r, k, v, g, r_k.reshape(-1), lnx_g, lnx_b), v_first


def _gdn_prep_kernel(x_ref, halo_ref, w_ref, o_ref, *, mode):
    x = x_ref[0].astype(F32)
    tr = x.shape[0]
    taps = w_ref.shape[0]
    nhalo = halo_ref.shape[1]
    halo = jnp.where(pl.program_id(1) > 0, halo_ref[0].astype(F32), 0.0)
    xc = jnp.concatenate([halo, x], axis=0)
    w = w_ref[...]
    y = sum(w[i:i + 1, :] * xc[nhalo - taps + 1 + i:nhalo - taps + 1 + i + tr] for i in range(taps))
    y = y * jax.nn.sigmoid(y)
    dh = GDN_HEAD_DIM
    for h in range(y.shape[1] // dh):
        seg = y[:, h * dh:(h + 1) * dh]
        if mode != "v":
            seg = seg * lax.rsqrt(jnp.sum(seg * seg, axis=-1, keepdims=True) + GDN_QK_EPS)
        if mode == "q":
            seg = seg * (dh ** -0.5)
        o_ref[0, :, h * dh:(h + 1) * dh] = seg.astype(o_ref.dtype)


def gdn_prep(p, conv_w, col0, ncols, mode, *, tr=512, tw=1024):
    bsz, t_len, _ = p.shape
    tr, tw = min(tr, t_len), min(tw, ncols)
    c0 = col0 // tw
    nhalo = 2 * SUBLANES
    rows = pl.BlockSpec((1, tr, tw), lambda b, i, j: (b, i, c0 + j))
    halo = pl.BlockSpec((1, nhalo, tw), lambda b, i, j: (b, jnp.maximum(i * (tr // nhalo) - 1, 0), c0 + j))
    return pl.pallas_call(
        functools.partial(_gdn_prep_kernel, mode=mode),
        out_shape=jax.ShapeDtypeStruct((bsz, t_len, ncols), BF16),
        grid=(bsz, t_len // tr, ncols // tw),
        in_specs=[rows, halo, pl.BlockSpec((conv_w.shape[0], tw), lambda b, i, j: (0, c0 + j))],
        out_specs=pl.BlockSpec((1, tr, tw), lambda b, i, j: (b, i, j)),
        compiler_params=_params("parallel", "parallel", "parallel"),
        name="gdn_prep_" + mode,
    )(p, p, conv_w)


def gated_deltanet(p_main, ba, conv_w, a_log, dt_bias, norm_w, key_width):
    hv = a_log.shape[0]
    conv_ch = conv_w.shape[1]
    q = gdn_prep(p_main, conv_w, 0, key_width, "q")
    k = gdn_prep(p_main, conv_w, key_width, key_width, "k")
    v = gdn_prep(p_main, conv_w, 2 * key_width, conv_ch - 2 * key_width, "v")
    beta = jax.nn.sigmoid(ba[..., :hv])
    g = -jnp.exp(a_log) * jax.nn.softplus(ba[..., hv:2 * hv] + dt_bias)
    return gdn_recurrence(q, k, v, g, beta, p_main, conv_ch, norm_w)


def _pad_cols(w, mult):
    n = w.shape[1]
    return jnp.pad(w, ((0, 0), (0, (-n) % mult)))


def kernel(x, ev_w_in, ev_shift, ev_w0, ev_w_up, ev_a0, ev_a_up, ev_g_up, ev_k_k, ev_k_a, ev_r_k, ev_lnx_g, ev_lnx_b, vres_w_down, vres_shift, vres_v0, vres_v_up, ev_w_out, od_w_in, od_conv, od_a_log, od_dt_bias, od_norm_w, od_w_out, ln1_g, ln1_b, mlp_w1, mlp_w2, ln2_g, ln2_b):
    bsz, t_len, d_model = x.shape
    depth = ln1_g.shape[0]
    alpha = (2 * depth) ** 0.25
    m = bsz * t_len
    rw_width = ev_w0.shape[1]
    sb_width = ev_w_out.shape[1] - rw_width
    sb_heads = sb_width // SB_HEAD_DIM
    sb_cols = 3 * sb_width
    gdn_hv = od_a_log.shape[1]
    val_width = gdn_hv * GDN_HEAD_DIM
    conv_ch = od_conv.shape[2]
    key_width = (conv_ch - val_width) // 2

    x = x.reshape(m, d_model)
    xb = x.astype(BF16)
    v_first = None
    for layer in range(depth):
        if layer % 2 == 0:
            e = layer // 2
            w_sb = ev_w_in[e][:, :sb_cols]
            w_rw = ev_w_in[e][:, sb_cols:]
            mu = ev_shift[e]
            v0 = v_up = None
            if e > 0:
                w_rw = jnp.concatenate([w_rw, vres_w_down[e - 1]], axis=1)
                mu = jnp.concatenate([mu, vres_shift[e - 1]])
                v0, v_up = vres_v0[e - 1], vres_v_up[e - 1]
            p_sb = matmul(xb, w_sb.astype(BF16), out_dtype=BF16, name="proj_sb")
            o_sb = sb_attention(p_sb.reshape(bsz, t_len, sb_cols), sb_heads)
            o_rw, v_first = rwkv7_group(xb, w_rw, mu, ev_w0[e], ev_w_up[e], ev_a0[e], ev_a_up[e], ev_g_up[e],
                                        ev_k_k[e], ev_k_a[e], ev_r_k[e], ev_lnx_g[e], ev_lnx_b[e],
                                        v_first, v0, v_up, bsz, t_len)
            x, xb = matmul_ln([o_sb.reshape(m, -1), o_rw.reshape(m, -1)], ev_w_out[e].astype(BF16), x,
                              ln1_g[layer], ln1_b[layer], alpha, tm=256, name="proj_out_ln")
        else:
            o = layer // 2
            w_main = od_w_in[o][:, :conv_ch + val_width]
            w_ba = od_w_in[o][:, conv_ch + val_width:]
            p_main = matmul(xb, w_main.astype(BF16), out_dtype=BF16, name="proj_gdn").reshape(bsz, t_len, -1)
            p_ba = matmul(xb, _pad_cols(w_ba, LANES).astype(BF16), name="proj_gdn_gates")[:, :2 * gdn_hv]
            mixed = gated_deltanet(p_main, p_ba.reshape(bsz, t_len, -1), od_conv[o], od_a_log[o],
                                   od_dt_bias[o], od_norm_w[o], key_width)
            x, xb = matmul_ln([mixed.reshape(m, -1)], od_w_out[o].astype(BF16), x, ln1_g[layer], ln1_b[layer],
                              alpha, tm=512, tk=1024, name="proj_out_ln")
        hid = matmul(xb, mlp_w1[layer].astype(BF16), act="relu2", out_dtype=BF16, name="mlp_up")
        x, xb = matmul_ln([hid], mlp_w2[layer].astype(BF16), x, ln2_g[layer], ln2_b[layer], alpha,
                          tm=512, tk=1024, name="mlp_down_ln")
    return x.reshape(bsz, t_len, d_model)
```
